```python
import math
import jax
import jax.numpy as jnp
from jax import lax
import numpy as np

D_MODEL = 4096
BATCH = 4
SEQ = 2048
DEPTH = 2
DEC_BATCH = 8
DEC_SEQ = 4
PAST_LEN = 16384
PAGE_SIZE = 128

N_MEM = 256
MEM_HEADS = 4
MEM_HEAD_DIM = 128
SSD_INNER = 2 * D_MODEL
SSD_HEAD_DIM = 64
SSD_HEADS = SSD_INNER // SSD_HEAD_DIM
SSD_GROUPS = 8
SSD_STATE = 128
SSD_CONV = 4
SSD_CHUNK = 128
CONV_DIM = SSD_INNER + 2 * SSD_GROUPS * SSD_STATE
SSD_IN_DIM = SSD_INNER + CONV_DIM + SSD_HEADS
SB_HEAD_DIM = 128
SB_HEADS = D_MODEL // SB_HEAD_DIM
SB_BLOCK = 128
N_EXPERTS = 128
TOP_K = 8
N_EXPERT_GROUPS = 8
TOPK_GROUPS = 4
D_EXPERT = 512
D_SHARED = 512
ROUTED_SCALE = 2.5
MOE_BLOCK = 128
MOE_SMALL_BLOCK = 8
N_MIXERS = 2
N_SSD_LAYERS = (DEPTH + 1) // 2
N_SB_LAYERS = DEPTH // 2
DN_ALPHA = (2 * DEPTH) ** 0.25
DN_BETA = (8 * DEPTH) ** -0.25
LN_EPS = 1e-5
RMS_EPS = 1e-5

kernel_name = 'hybrid_ssd_stickbreak_moe_step'


def layer_norm(x, g, b):
    xf = x.astype(jnp.float32)
    mu = jnp.mean(xf, axis=-1, keepdims=True)
    var = jnp.mean(jnp.square(xf - mu), axis=-1, keepdims=True)
    return ((xf - mu) * lax.rsqrt(var + LN_EPS) * g + b).astype(x.dtype)


def segsum(a):
    t = a.shape[-1]
    a_rep = jnp.broadcast_to(a[..., :, None], a.shape + (t,))
    strict = jnp.tril(jnp.ones((t, t), bool), -1)
    s = jnp.cumsum(jnp.where(strict, a_rep, 0.0), axis=-2)
    return jnp.where(jnp.tril(jnp.ones((t, t), bool)), s, -jnp.inf)


def ssd_scan(x, dt, a_neg, b, c, h0):
    bt, L, H, P = x.shape
    G, N = b.shape[-2:]
    R = H // G
    q = min(SSD_CHUNK, L)
    nc = -(-L // q)
    pad = nc * q - L
    f32 = jnp.float32

    def pad_time(t):
        return jnp.pad(t, [(0, 0), (0, pad)] + [(0, 0)] * (t.ndim - 2))

    xdt = pad_time(x.astype(f32) * dt[..., None]).reshape(bt, nc, q, G, R, P)
    da = jnp.moveaxis(pad_time(dt * a_neg).reshape(bt, nc, q, G, R), 2, -1)
    bs = pad_time(b.astype(f32)).reshape(bt, nc, q, G, N)
    cs = pad_time(c.astype(f32)).reshape(bt, nc, q, G, N)
    da_cum = jnp.cumsum(da, axis=-1)
    w = jnp.einsum('bclgn,bcsgn->bcgls', cs, bs)[:, :, :, None] * jnp.exp(segsum(da))
    y = jnp.einsum('bcgrls,bcsgrp->bclgrp', w, xdt)
    to_end = jnp.moveaxis(jnp.exp(da_cum[..., -1:] - da_cum), -1, 2)[..., None]
    states = jnp.einsum('bclgn,bclgrp->bcgrpn', bs, xdt * to_end)
    states = jnp.concatenate([h0.astype(f32).reshape(bt, 1, G, R, P, N), states], axis=1)
    chunk_tot = jnp.pad(jnp.moveaxis(da_cum[..., -1], 1, -1), [(0, 0)] * 3 + [(1, 0)])
    states = jnp.einsum('bgrzc,bcgrpn->bzgrpn', jnp.exp(segsum(chunk_tot)), states)
    from_start = jnp.moveaxis(jnp.exp(da_cum), -1, 2)[..., None]
    y = y + jnp.einsum('bclgn,bcgrpn->bclgrp', cs, states[:, :-1]) * from_start
    y = y.reshape(bt, nc * q, H, P)[:, :L]
    return y, states[:, -1].reshape(bt, H, P, N)


def ssd_mixer(x, conv_buf, h0, w_in, conv_w, conv_b, dt_bias, a_log, d_skip, norm_g, w_out):
    bt, L, _ = x.shape
    z, xbc, dt = jnp.split(x @ w_in, [SSD_INNER, SSD_INNER + CONV_DIM], axis=-1)
    ext = jnp.concatenate([conv_buf.astype(xbc.dtype), xbc], axis=1)
    conv = conv_b + ext[:, SSD_CONV - 1:] * conv_w[SSD_CONV - 1]
    for tap in range(SSD_CONV - 1):
        conv = conv + ext[:, tap:tap + L] * conv_w[tap]
    new_conv = ext[:, L:]
    xbc = jax.nn.silu(conv)
    xs, bm, cm = jnp.split(xbc, [SSD_INNER, SSD_INNER + SSD_GROUPS * SSD_STATE], axis=-1)
    xs = xs.reshape(bt, L, SSD_HEADS, SSD_HEAD_DIM)
    bm = bm.reshape(bt, L, SSD_GROUPS, SSD_STATE)
    cm = cm.reshape(bt, L, SSD_GROUPS, SSD_STATE)
    dt = jax.nn.softplus(dt.astype(jnp.float32) + dt_bias.astype(jnp.float32))
    a_neg = -jnp.exp(a_log.astype(jnp.float32))
    y, h_final = ssd_scan(xs, dt, a_neg, bm, cm, h0)
    y = y + d_skip.astype(jnp.float32)[:, None] * xs.astype(jnp.float32)
    y = (y.reshape(bt, L, SSD_GROUPS, SSD_INNER // SSD_GROUPS)
         * jax.nn.silu(z.astype(jnp.float32)).reshape(bt, L, SSD_GROUPS, SSD_INNER // SSD_GROUPS))
    y = y * lax.rsqrt(jnp.mean(jnp.square(y), axis=-1, keepdims=True) + RMS_EPS)
    y = (y.reshape(bt, L, SSD_INNER) * norm_g).astype(x.dtype)
    return y @ w_out, new_conv, h_final.astype(h0.dtype)


def stick_breaking_block(q, k, v, mask, log_surv, bias):
    z = (jnp.einsum('bqhd,bkhd->bhqk', q, k, preferred_element_type=jnp.float32) * (SB_HEAD_DIM ** -0.5)
         + bias.astype(jnp.float32)[None, :, None, None])
    m = mask[None, None]
    log_keep = jnp.where(m, jax.nn.log_sigmoid(-z), 0.0)
    later = lax.cumsum(log_keep, axis=3, reverse=True) - log_keep
    a = jnp.where(m, jnp.exp(jax.nn.log_sigmoid(z) + later + log_surv[..., None]), 0.0)
    out = jnp.einsum('bhqk,bkhd->bqhd', a, v.astype(jnp.float32))
    return out, log_surv + jnp.sum(log_keep, axis=3)


def sb_qkv(x, w_qkv):
    b, L, _ = x.shape
    qkv = (x @ w_qkv).reshape(b, L, 3, SB_HEADS, SB_HEAD_DIM)
    return qkv[:, :, 0], qkv[:, :, 1], qkv[:, :, 2]


def sb_prompt(x, w_qkv, w_o, bias):
    b, S, _ = x.shape
    q, k, v = sb_qkv(x, w_qkv)
    kpos = jnp.arange(S)

    def one_query_block(i):
        start = i * SB_BLOCK
        qb = lax.dynamic_slice_in_dim(q, start, SB_BLOCK, axis=1)
        qpos = start + jnp.arange(SB_BLOCK)
        out, _ = stick_breaking_block(qb, k, v, kpos[None, :] < qpos[:, None],
                                      jnp.zeros((b, SB_HEADS, SB_BLOCK), jnp.float32), bias)
        return out

    o = lax.map(one_query_block, jnp.arange(S // SB_BLOCK))
    o = jnp.moveaxis(o, 0, 1).reshape(b, S, D_MODEL)
    return o.astype(x.dtype) @ w_o, k, v


def sb_sample(x, pool_k, pool_v, layer, page_table, w_qkv, w_o, bias):
    b, L, _ = x.shape
    q, k, v = sb_qkv(x, w_qkv)
    pos = jnp.arange(L)
    out, surv = stick_breaking_block(q, k, v, pos[None, :] < pos[:, None],
                                     jnp.zeros((b, SB_HEADS, L), jnp.float32), bias)
    past_mask = jnp.ones((L, PAGE_SIZE), bool)

    def step(carry, pages):
        acc, surv = carry
        o, surv = stick_breaking_block(q, pool_k[layer, pages], pool_v[layer, pages], past_mask, surv, bias)
        return (acc + o, surv), None

    (out, _), _ = lax.scan(step, (out, surv), page_table.T, reverse=True)
    return out.reshape(b, L, D_MODEL).astype(x.dtype) @ w_o, k, v


def mem_kv(mem, w_kv):
    b = mem.shape[0]
    kv = (mem @ w_kv).reshape(b, N_MEM, 2, MEM_HEADS, MEM_HEAD_DIM)
    return kv[:, :, 0], kv[:, :, 1]


def mem_attend(x, mk, mv, w_q, w_o):
    b, L, _ = x.shape
    q = (x @ w_q).reshape(b, L, MEM_HEADS, MEM_HEAD_DIM)
    s = jnp.einsum('blhd,bmhd->bhlm', q, mk, preferred_element_type=jnp.float32) * (MEM_HEAD_DIM ** -0.5)
    p = jax.nn.softmax(s, axis=-1)
    o = jnp.einsum('bhlm,bmhd->blhd', p.astype(mv.dtype), mv).reshape(b, L, MEM_HEADS * MEM_HEAD_DIM)
    return o @ w_o


def route(x2, w_router, e_bias):
    t = x2.shape[0]
    scores = jax.nn.sigmoid(jnp.einsum('td,de->te', x2, w_router, preferred_element_type=jnp.float32))
    sel = scores + e_bias.astype(jnp.float32)
    per_group = N_EXPERTS // N_EXPERT_GROUPS
    group_score = jnp.sum(lax.top_k(sel.reshape(t, N_EXPERT_GROUPS, per_group), 2)[0], axis=-1)
    _, top_groups = lax.top_k(group_score, TOPK_GROUPS)
    keep = jnp.repeat(jnp.sum(jax.nn.one_hot(top_groups, N_EXPERT_GROUPS), axis=-2) > 0, per_group, axis=-1)
    _, idx = lax.top_k(jnp.where(keep, sel, -jnp.inf), TOP_K)
    w = jnp.take_along_axis(scores, idx, axis=-1)
    return idx, w / jnp.sum(w, axis=-1, keepdims=True) * ROUTED_SCALE


def routed_experts(x2, idx, gate, layer, w_gate, w_up, w_down):
    t, d = x2.shape
    n = t * TOP_K
    blk = MOE_BLOCK if n >= MOE_BLOCK * N_EXPERTS else MOE_SMALL_BLOCK
    n_blocks = (n + N_EXPERTS * (blk - 1)) // blk
    e_flat = idx.reshape(n)
    order = jnp.argsort(e_flat)
    e_sorted = e_flat[order]
    counts = jnp.bincount(e_flat, length=N_EXPERTS)
    padded = (counts + blk - 1) // blk * blk
    ends = jnp.cumsum(padded)
    dest = (ends - padded)[e_sorted] + jnp.arange(n) - (jnp.cumsum(counts) - counts)[e_sorted]
    tok = jnp.repeat(jnp.arange(t, dtype=jnp.int32), TOP_K)[order]
    slot_tok = jnp.zeros((n_blocks * blk,), jnp.int32).at[dest].set(tok)
    slot_gate = jnp.zeros((n_blocks * blk,), gate.dtype).at[dest].set(gate.reshape(n)[order])
    block_expert = jnp.minimum(jnp.searchsorted(ends, jnp.arange(n_blocks) * blk, side='right'), N_EXPERTS - 1)

    def one_block(args):
        e, toks, g = args
        xb = x2[toks]
        hb = jax.nn.silu(xb @ w_gate[layer, e]) * (xb @ w_up[layer, e])
        return (hb @ w_down[layer, e]) * g[:, None].astype(xb.dtype)

    yb = lax.map(one_block, (block_expert, slot_tok.reshape(n_blocks, blk), slot_gate.reshape(n_blocks, blk)))
    return jnp.zeros_like(x2).at[slot_tok].add(yb.reshape(n_blocks * blk, d))


def moe(x, layer, router_w, router_bias, w_gate, w_up, w_down, s_gate, s_up, s_down):
    b, L, d = x.shape
    x2 = x.reshape(b * L, d)
    idx, gate = route(x2, router_w[layer], router_bias[layer])
    routed = routed_experts(x2, idx, gate, layer, w_gate, w_up, w_down)
    shared = (jax.nn.silu(x2 @ s_gate[layer]) * (x2 @ s_up[layer])) @ s_down[layer]
    return (routed + shared).reshape(b, L, d)


def setup_inputs(seed: int = 0) -> dict:
    key = jax.random.key(seed)
    ks = iter(jax.random.split(key, 48))
    f32 = jnp.float32
    n_pages = PAST_LEN // PAGE_SIZE
    n_used = DEC_BATCH * n_pages
    n_phys = n_used + max(1, n_used // 4)

    def nrm(shape, scale=1.0):
        return jax.random.normal(next(ks), shape, f32) * scale

    def gain(shape):
        return 1.0 + nrm(shape, 0.02)

    dt0 = jnp.exp(jax.random.uniform(next(ks), (N_SSD_LAYERS, SSD_HEADS), f32, math.log(1e-3), math.log(1e-1)))
    dt_bias = dt0 + jnp.log(-jnp.expm1(-dt0))
    a_log = jnp.log(jax.random.uniform(next(ks), (N_SSD_LAYERS, SSD_HEADS), f32, 1.0, 16.0))
    page_table = jax.random.permutation(next(ks), n_phys)[:n_used].reshape(DEC_BATCH, n_pages).astype(jnp.int32)
    sb_logit_bias = jax.random.uniform(next(ks), (N_SB_LAYERS, SB_HEADS), f32, -10.0, -5.0)
    return {
        'x_prompt': nrm((BATCH, SEQ, D_MODEL)),
        'x_sample': nrm((DEC_BATCH, DEC_SEQ, D_MODEL)),
        'mem_prompt': nrm((BATCH, N_MEM, D_MODEL)),
        'cache_ssm_state': nrm((N_SSD_LAYERS, DEC_BATCH, SSD_HEADS, SSD_HEAD_DIM, SSD_STATE), 0.1),
        'cache_conv': nrm((N_SSD_LAYERS, DEC_BATCH, SSD_CONV - 1, CONV_DIM)),
        'cache_sb_k': nrm((N_SB_LAYERS, n_phys, PAGE_SIZE, SB_HEADS, SB_HEAD_DIM)),
        'cache_sb_v': nrm((N_SB_LAYERS, n_phys, PAGE_SIZE, SB_HEADS, SB_HEAD_DIM)),
        'cache_mem_k': nrm((DEPTH, DEC_BATCH, N_MEM, MEM_HEADS, MEM_HEAD_DIM)),
        'cache_mem_v': nrm((DEPTH, DEC_BATCH, N_MEM, MEM_HEADS, MEM_HEAD_DIM)),
        'page_table': page_table,
        'ssd_w_in': nrm((N_SSD_LAYERS, D_MODEL, SSD_IN_DIM), D_MODEL ** -0.5),
        'ssd_conv_w': nrm((N_SSD_LAYERS, SSD_CONV, CONV_DIM), SSD_CONV ** -0.5),
        'ssd_conv_b': nrm((N_SSD_LAYERS, CONV_DIM), 0.02),
        'ssd_dt_bias': dt_bias,
        'ssd_a_log': a_log,
        'ssd_d': gain((N_SSD_LAYERS, SSD_HEADS)),
        'ssd_norm_g': gain((N_SSD_LAYERS, SSD_INNER)),
        'ssd_w_out': nrm((N_SSD_LAYERS, SSD_INNER, D_MODEL), DN_BETA * SSD_INNER ** -0.5),
        'sb_w_qkv': nrm((N_SB_LAYERS, D_MODEL, 3 * D_MODEL), D_MODEL ** -0.5),
        'sb_w_o': nrm((N_SB_LAYERS, D_MODEL, D_MODEL), DN_BETA * D_MODEL ** -0.5),
        'sb_logit_bias': sb_logit_bias,
        'mem_w_q': nrm((DEPTH, D_MODEL, MEM_HEADS * MEM_HEAD_DIM), D_MODEL ** -0.5),
        'mem_w_kv': nrm((DEPTH, D_MODEL, 2 * MEM_HEADS * MEM_HEAD_DIM), D_MODEL ** -0.5),
        'mem_w_o': nrm((DEPTH, MEM_HEADS * MEM_HEAD_DIM, D_MODEL), DN_BETA * (MEM_HEADS * MEM_HEAD_DIM) ** -0.5),
        'router_w': nrm((DEPTH, D_MODEL, N_EXPERTS), D_MODEL ** -0.5),
        'router_bias': nrm((DEPTH, N_EXPERTS), 0.01),
        'moe_w_gate': nrm((DEPTH, N_EXPERTS, D_MODEL, D_EXPERT), D_MODEL ** -0.5),
        'moe_w_up': nrm((DEPTH, N_EXPERTS, D_MODEL, D_EXPERT), D_MODEL ** -0.5),
        'moe_w_down': nrm((DEPTH, N_EXPERTS, D_EXPERT, D_MODEL), DN_BETA * D_EXPERT ** -0.5),
        'shared_w_gate': nrm((DEPTH, D_MODEL, D_SHARED), D_MODEL ** -0.5),
        'shared_w_up': nrm((DEPTH, D_MODEL, D_SHARED), D_MODEL ** -0.5),
        'shared_w_down': nrm((DEPTH, D_SHARED, D_MODEL), DN_BETA * D_SHARED ** -0.5),
        'ln_g': gain((DEPTH, 3, D_MODEL)),
        'ln_b': nrm((DEPTH, 3, D_MODEL), 0.02),
    }


def reference(x_prompt, x_sample, mem_prompt, cache_ssm_state, cache_conv, cache_sb_k, cache_sb_v,
              cache_mem_k, cache_mem_v, page_table,
              ssd_w_in, ssd_conv_w, ssd_conv_b, ssd_dt_bias, ssd_a_log, ssd_d, ssd_norm_g, ssd_w_out,
              sb_w_qkv, sb_w_o, sb_logit_bias, mem_w_q, mem_w_kv, mem_w_o,
              router_w, router_bias, moe_w_gate, moe_w_up, moe_w_down,
              shared_w_gate, shared_w_up, shared_w_down, ln_g, ln_b):

    def ssd_params(j):
        return (ssd_w_in[j], ssd_conv_w[j], ssd_conv_b[j], ssd_dt_bias[j], ssd_a_log[j],
                ssd_d[j], ssd_norm_g[j], ssd_w_out[j])

    def layer_tail(l, h, mix, mk, mv):
        h = layer_norm(DN_ALPHA * h + mix, ln_g[l, 0], ln_b[l, 0])
        h = layer_norm(DN_ALPHA * h + mem_attend(h, mk, mv, mem_w_q[l], mem_w_o[l]), ln_g[l, 1], ln_b[l, 1])
        ffn = moe(h, l, router_w, router_bias, moe_w_gate, moe_w_up, moe_w_down,
                  shared_w_gate, shared_w_up, shared_w_down)
        return layer_norm(DN_ALPHA * h + ffn, ln_g[l, 2], ln_b[l, 2])

    h = x_prompt
    bp = x_prompt.shape[0]
    ssm_p, conv_p, k_p, v_p, mk_p, mv_p = [], [], [], [], [], []
    for l in range(DEPTH):
        j = l // N_MIXERS
        if l % N_MIXERS == 0:
            conv0 = jnp.zeros((bp, SSD_CONV - 1, CONV_DIM), h.dtype)
            ssm0 = jnp.zeros((bp, SSD_HEADS, SSD_HEAD_DIM, SSD_STATE), h.dtype)
            mix, conv_new, ssm_new = ssd_mixer(h, conv0, ssm0, *ssd_params(j))
            conv_p.append(conv_new)
            ssm_p.append(ssm_new)
        else:
            mix, k_new, v_new = sb_prompt(h, sb_w_qkv[j], sb_w_o[j], sb_logit_bias[j])
            k_p.append(k_new)
            v_p.append(v_new)
        mk, mv = mem_kv(mem_prompt, mem_w_kv[l])
        mk_p.append(mk)
        mv_p.append(mv)
        h = layer_tail(l, h, mix, mk, mv)
    y_prompt = h

    g = x_sample
    ssm_s, conv_s, k_s, v_s = [], [], [], []
    for l in range(DEPTH):
        j = l // N_MIXERS
        if l % N_MIXERS == 0:
            mix, conv_new, ssm_new = ssd_mixer(g, cache_conv[j], cache_ssm_state[j], *ssd_params(j))
            conv_s.append(conv_new)
            ssm_s.append(ssm_new)
        else:
            mix, k_new, v_new = sb_sample(g, cache_sb_k, cache_sb_v, j, page_table, sb_w_qkv[j], sb_w_o[j],
                                          sb_logit_bias[j])
            k_s.append(k_new)
            v_s.append(v_new)
        g = layer_tail(l, g, mix, cache_mem_k[l], cache_mem_v[l])
    y_sample = g

    ssm_state_prompt = jnp.stack(ssm_p)
    conv_state_prompt = jnp.stack(conv_p)
    sb_k_prompt = jnp.stack(k_p)
    sb_v_prompt = jnp.stack(v_p)
    mem_k_prompt = jnp.stack(mk_p)
    mem_v_prompt = jnp.stack(mv_p)
    ssm_state_sample = jnp.stack(ssm_s)
    conv_state_sample = jnp.stack(conv_s)
    sb_k_sample = jnp.stack(k_s)
    sb_v_sample = jnp.stack(v_s)
    return (y_prompt, y_sample, ssm_state_prompt, conv_state_prompt, sb_k_prompt, sb_v_prompt,
            mem_k_prompt, mem_v_prompt, ssm_state_sample, conv_state_sample, sb_k_sample, sb_v_sample)
```

```python
import functools

import jax
import jax.numpy as jnp
import numpy as np
from jax import lax
from jax.experimental import pallas as pl
from jax.experimental.pallas import tpu as pltpu

F32 = jnp.float32
BF16 = jnp.bfloat16

SSD_CHUNK = 128
TOP_K = 8
N_EXPERT_GROUPS = 8
TOPK_GROUPS = 4
ROUTED_SCALE = 2.5
N_MIXERS = 2
LN_EPS = 1e-5
RMS_EPS = 1e-5

SUBLANES = 8
LANES = 128
VMEM_LIMIT_BYTES = 56 * 1024 * 1024
MATMUL_BLOCK_BYTES = 8 * 1024 * 1024
SAMPLE_ROWS = 8
MOE_ROWS = 128


def _params(*sem):
    return pltpu.CompilerParams(dimension_semantics=sem, vmem_limit_bytes=VMEM_LIMIT_BYTES)


def _largest_tile(n, cap, quantum):
    if n <= cap:
        return n
    t = cap - cap % quantum
    while t > quantum and n % t:
        t -= quantum
    assert n % t == 0, (n, cap, quantum)
    return t


def _sigmoid(x):
    return 1.0 / (1.0 + jnp.exp(-x))


def _silu(x):
    return x * _sigmoid(x)


def _softplus_neg_abs(x):
    return jnp.log1p(jnp.exp(-jnp.abs(x)))


def _split_bf16(x):
    hi = x.astype(BF16)
    lo = (x - hi.astype(F32)).astype(BF16)
    return hi, lo


def _layer_norm_rows(x, g, b):
    mu = jnp.mean(x, axis=-1, keepdims=True)
    xc = x - mu
    var = jnp.mean(xc * xc, axis=-1, keepdims=True)
    return xc * lax.rsqrt(var + LN_EPS) * g + b


def _matmul_kernel(x_ref, w_ref, o_ref):
    o_ref[...] = jnp.dot(x_ref[...], w_ref[...], preferred_element_type=F32).astype(o_ref.dtype)


def _matmul(x, w, out_dtype=F32):
    m, k = x.shape
    n = w.shape[1]
    cap = max(MATMUL_BLOCK_BYTES // (2 * k), LANES)
    tm = _largest_tile(m, cap, SUBLANES)
    tn = _largest_tile(n, cap, LANES)
    return pl.pallas_call(
        _matmul_kernel,
        grid=(m // tm, n // tn),
        in_specs=[pl.BlockSpec((tm, k), lambda i, j: (i, 0)),
                  pl.BlockSpec((k, tn), lambda i, j: (0, j))],
        out_specs=pl.BlockSpec((tm, tn), lambda i, j: (i, j)),
        out_shape=jax.ShapeDtypeStruct((m, n), out_dtype),
        compiler_params=_params("parallel", "arbitrary"),
        name="dense_matmul",
    )(x, w)


def _res_ln_kernel(h_ref, m_ref, g_ref, b_ref, o_ref, ob_ref, *, alpha):
    y = _layer_norm_rows(alpha * h_ref[...] + m_ref[...], g_ref[...], b_ref[...])
    o_ref[...] = y
    ob_ref[...] = y.astype(BF16)


def _res_ln(h, mix, g, b, alpha):
    t, d = h.shape
    tm = _largest_tile(t, 256, SUBLANES)
    row = pl.BlockSpec((tm, d), lambda i: (i, 0))
    vec = pl.BlockSpec((1, d), lambda i: (0, 0))
    return pl.pallas_call(
        functools.partial(_res_ln_kernel, alpha=alpha),
        grid=(t // tm,),
        in_specs=[row, row, vec, vec],
        out_specs=[row, row],
        out_shape=[jax.ShapeDtypeStruct((t, d), F32), jax.ShapeDtypeStruct((t, d), BF16)],
        compiler_params=_params("parallel"),
        name="residual_layernorm",
    )(h, mix, g.reshape(1, d), b.reshape(1, d))


def _ssd_scan_kernel(z_ref, xs_ref, bm_ref, cm_ref, dt_ref,
                     c0x_ref, c0b_ref, c0c_ref, h0_ref,
                     cwx_ref, cwb_ref, cwc_ref, cbx_ref, cbb_ref, cbc_ref,
                     dtb_ref, alog_ref, dexp_ref, ng_ref, su2_ref,
                     y_ref, hout_ref,
                     extx_ref, extb_ref, extc_ref, keepx_ref, keepb_ref, keepc_ref,
                     dtt_ref, dat_ref, yacc_ref,
                     *, q, l_valid, head_dim, heads_per_group):
    g = pl.program_id(1)
    c = pl.program_id(2)
    taps = cwx_ref.shape[0]
    halo = SUBLANES
    n_state = bm_ref.shape[-1]
    gw = xs_ref.shape[-1]
    pair_w = 2 * head_dim

    @pl.when(c == 0)
    def _():
        keepx_ref[...] = c0x_ref[0]
        keepb_ref[...] = c0b_ref[0]
        keepc_ref[...] = c0c_ref[0]
        hout_ref[0] = h0_ref[0]

    def conv_slab(ext_ref, keep_ref, src_ref, cw_ref, cb_ref):
        width = ext_ref.shape[-1]
        ext_ref[0:halo, :] = keep_ref[...]
        ext_ref[halo:halo + q, :] = src_ref[0]
        keep_ref[...] = ext_ref[q:q + halo, :]
        ct = min(width, 2 * LANES)
        for j in range(width // ct):
            cs = slice(j * ct, (j + 1) * ct)
            acc = cb_ref[:, cs] + ext_ref[halo:halo + q, cs] * cw_ref[taps - 1:taps, cs]
            for tap in range(taps - 1):
                lo = halo - (taps - 1) + tap
                acc = acc + ext_ref[lo:lo + q, cs] * cw_ref[tap:tap + 1, cs]
            ext_ref[halo:halo + q, cs] = _silu(acc)

    conv_slab(extx_ref, keepx_ref, xs_ref, cwx_ref, cbx_ref)
    conv_slab(extb_ref, keepb_ref, bm_ref, cwb_ref, cbb_ref)
    conv_slab(extc_ref, keepc_ref, cm_ref, cwc_ref, cbc_ref)

    dtv = dt_ref[0] + dtb_ref[...]
    dtv = jnp.maximum(dtv, 0.0) + _softplus_neg_abs(dtv)
    rows = lax.broadcasted_iota(jnp.int32, dtv.shape, 0) + c * q
    dtv = jnp.where(rows < l_valid, dtv, 0.0)
    dtt = dtv.T
    dtt_ref[...] = dtt
    dat_ref[...] = dtt * (-jnp.exp(alog_ref[...]))

    bg = extb_ref[halo:halo + q, :]
    cg = extc_ref[halo:halo + q, :]
    cb = lax.dot_general(cg.astype(BF16), bg.astype(BF16), (((1,), (1,)), ((), ())),
                         preferred_element_type=F32)
    bgt = bg.T
    li = lax.broadcasted_iota(jnp.int32, (q, q), 0)
    si = lax.broadcasted_iota(jnp.int32, (q, q), 1)
    tri = si <= li
    tri_f = tri.astype(F32)
    first_head = lax.broadcasted_iota(jnp.int32, (1, pair_w), 1) < head_dim
    su2 = su2_ref[...]

    for pp in range(heads_per_group // 2):
        ls = slice(pp * pair_w, (pp + 1) * pair_w)
        xp = extx_ref[halo:halo + q, ls].astype(BF16)
        sp = hout_ref[0, :, ls]
        rhs_y = jnp.concatenate([xp, sp.astype(BF16)], axis=0)
        ys, us, decs = [], [], []
        for hh in range(2):
            h = g * heads_per_group + pp * 2 + hh
            da_row = dat_ref[pl.ds(h, 1), :]
            dt_row = dtt_ref[pl.ds(h, 1), :]
            hi, lo = _split_bf16(tri_f * da_row)
            dfull = jnp.dot(jnp.concatenate([hi, lo], axis=1), su2, preferred_element_type=F32)
            dm = dfull[:, :q]
            cumcol = dfull[:, q:]
            decay = jnp.where(tri, jnp.exp(dm), 0.0)
            to_end = jnp.exp(dm[q - 1:q, :])
            w_intra = (cb * decay * dt_row).astype(BF16)
            c_inter = (cg * jnp.exp(cumcol[:, :n_state])).astype(BF16)
            ys.append(jnp.dot(jnp.concatenate([w_intra, c_inter], axis=1), rhs_y,
                              preferred_element_type=F32))
            us.append(jnp.dot((bgt * (to_end * dt_row)).astype(BF16), xp,
                              preferred_element_type=F32))
            decs.append(jnp.exp(cumcol[q - 1:q, :pair_w]))
        yacc_ref[:, ls] = jnp.where(first_head, ys[0], ys[1])
        hout_ref[0, :, ls] = (sp * jnp.where(first_head, decs[0], decs[1])
                              + jnp.where(first_head, us[0], us[1]))

    yg = yacc_ref[...] + dexp_ref[...] * extx_ref[halo:halo + q, :]
    yg = yg * _silu(z_ref[0])
    ms = jnp.mean(yg * yg, axis=-1, keepdims=True)
    y_ref[0] = (yg * lax.rsqrt(ms + RMS_EPS) * ng_ref[...]).astype(y_ref.dtype)


def _ssd_scan(z, xbc, dtr, conv0, h0, conv_w, conv_b, dt_bias, a_log, d_skip, norm_g, l_valid):
    bt, lp, inner = z.shape
    cd = xbc.shape[-1]
    n_heads = dtr.shape[-1]
    n_state = h0.shape[1]
    head_dim = inner // n_heads
    n_groups = (cd - inner) // (2 * n_state)
    heads_per_group = n_heads // n_groups
    gw = heads_per_group * head_dim
    q = SSD_CHUNK
    nc = lp // q
    assert lp % q == 0 and (nc == 1 or l_valid == lp)
    assert gw % n_state == 0 and inner % n_state == 0 and heads_per_group % 2 == 0
    taps = conv_w.shape[0]
    b_blk = inner // n_state
    c_blk = (inner + n_groups * n_state) // n_state

    su = (np.arange(q)[:, None] > np.arange(q)[None, :]).astype(np.float32)
    su = np.concatenate([su, np.ones((q, max(n_state, 2 * head_dim)), np.float32)], axis=1)
    su2 = jnp.asarray(np.concatenate([su, su], axis=0), BF16)

    d_exp = jnp.repeat(d_skip.astype(F32), head_dim).reshape(1, inner)

    def xs_map(b, g, c):
        return (b, c, g)

    in_specs = [
        pl.BlockSpec((1, q, gw), xs_map),
        pl.BlockSpec((1, q, gw), xs_map),
        pl.BlockSpec((1, q, n_state), lambda b, g, c: (b, c, b_blk + g)),
        pl.BlockSpec((1, q, n_state), lambda b, g, c: (b, c, c_blk + g)),
        pl.BlockSpec((1, q, n_heads), lambda b, g, c: (b, c, 0)),
        pl.BlockSpec((1, SUBLANES, gw), lambda b, g, c: (b, 0, g)),
        pl.BlockSpec((1, SUBLANES, n_state), lambda b, g, c: (b, 0, b_blk + g)),
        pl.BlockSpec((1, SUBLANES, n_state), lambda b, g, c: (b, 0, c_blk + g)),
        pl.BlockSpec((1, n_state, gw), lambda b, g, c: (b, 0, g)),
        pl.BlockSpec((taps, gw), lambda b, g, c: (0, g)),
        pl.BlockSpec((taps, n_state), lambda b, g, c: (0, b_blk + g)),
        pl.BlockSpec((taps, n_state), lambda b, g, c: (0, c_blk + g)),
        pl.BlockSpec((1, gw), lambda b, g, c: (0, g)),
        pl.BlockSpec((1, n_state), lambda b, g, c: (0, b_blk + g)),
        pl.BlockSpec((1, n_state), lambda b, g, c: (0, c_blk + g)),
        pl.BlockSpec((1, n_heads), lambda b, g, c: (0, 0)),
        pl.BlockSpec((n_heads, 1), lambda b, g, c: (0, 0)),
        pl.BlockSpec((1, gw), lambda b, g, c: (0, g)),
        pl.BlockSpec((1, gw), lambda b, g, c: (0, g)),
        pl.BlockSpec(su2.shape, lambda b, g, c: (0, 0)),
    ]
    out_specs = [pl.BlockSpec((1, q, gw), xs_map),
                 pl.BlockSpec((1, n_state, gw), lambda b, g, c: (b, 0, g))]
    scratch = [pltpu.VMEM((SUBLANES + q, gw), F32), pltpu.VMEM((SUBLANES + q, n_state), F32),
               pltpu.VMEM((SUBLANES + q, n_state), F32),
               pltpu.VMEM((SUBLANES, gw), F32), pltpu.VMEM((SUBLANES, n_state), F32),
               pltpu.VMEM((SUBLANES, n_state), F32),
               pltpu.VMEM((n_heads, q), F32), pltpu.VMEM((n_heads, q), F32),
               pltpu.VMEM((q, gw), F32)]
    cb2 = conv_b.reshape(1, cd)
    return pl.pallas_call(
        functools.partial(_ssd_scan_kernel, q=q, l_valid=l_valid, head_dim=head_dim,
                          heads_per_group=heads_per_group),
        grid=(bt, n_groups, nc),
        in_specs=in_specs,
        out_specs=out_specs,
        out_shape=[jax.ShapeDtypeStruct((bt, lp, inner), BF16),
                   jax.ShapeDtypeStruct((bt, n_state, inner), F32)],
        scratch_shapes=scratch,
        compiler_params=_params("parallel", "arbitrary", "arbitrary"),
        name="ssd_scan",
    )(z, xbc, xbc, xbc, dtr, conv0, conv0, conv0, h0, conv_w, conv_w, conv_w, cb2, cb2, cb2,
      dt_bias.reshape(1, n_heads), a_log.reshape(n_heads, 1), d_exp, norm_g.reshape(1, inner), su2)


def _ssd_mixer(x, l_valid, conv_buf, h0, w_z, w_xbc, w_dt, conv_w, conv_b, dt_bias, a_log, d_skip,
               norm_g, w_out):
    bt, l, d = x.shape
    _, n_heads, head_dim, n_state = h0.shape
    inner = n_heads * head_dim
    xb = x.astype(BF16).reshape(bt * l, d)
    z = _matmul(xb, w_z).reshape(bt, l, inner)
    xbc = _matmul(xb, w_xbc).reshape(bt, l, -1)
    dtr = _matmul(xb, w_dt).reshape(bt, l, n_heads)
    new_conv = jnp.concatenate([conv_buf, xbc[:, :l_valid]], axis=1)[:, l_valid:]

    lp = -(-l // SSD_CHUNK) * SSD_CHUNK
    pad = ((0, 0), (0, lp - l), (0, 0))
    conv0 = jnp.pad(conv_buf, ((0, 0), (SUBLANES - conv_buf.shape[1], 0), (0, 0)))
    h0t = jnp.transpose(h0.reshape(bt, inner, n_state), (0, 2, 1))
    y, ht = _ssd_scan(jnp.pad(z, pad), jnp.pad(xbc, pad), jnp.pad(dtr, pad), conv0, h0t,
                      conv_w, conv_b, dt_bias, a_log, d_skip, norm_g, l_valid)
    mix = _matmul(y[:, :l].reshape(bt * l, inner), w_out).reshape(bt, l, d)
    h_final = jnp.transpose(ht, (0, 2, 1)).reshape(bt, n_heads, head_dim, n_state)
    return mix, new_conv, h_final


def _stick_breaking_weights(z, mask, surv, cum_op, key_axis):
    tail = _softplus_neg_abs(z)
    log_beta = jnp.minimum(z, 0.0) - tail
    log_keep = -jnp.maximum(z, 0.0) - tail
    if mask is not None:
        log_keep = jnp.where(mask, log_keep, 0.0)
    hi, lo = _split_bf16(log_keep)
    if key_axis == 1:
        later = jnp.dot(jnp.concatenate([hi, lo], axis=1), cum_op, preferred_element_type=F32)
    else:
        later = jnp.dot(cum_op, jnp.concatenate([hi, lo], axis=0), preferred_element_type=F32)
    a = jnp.exp(log_beta + later + surv)
    if mask is not None:
        a = jnp.where(mask, a, 0.0)
    return a, jnp.sum(log_keep, axis=key_axis, keepdims=True)


def _sb_prompt_kernel(bias_ref, q_ref, k_ref, v_ref, cum_ref, o_ref, *, tq, scale):
    h = pl.program_id(1)
    qi = pl.program_id(2)
    bias = bias_ref[h]
    qb = q_ref[0].astype(BF16)
    cum_op = cum_ref[...]
    d = q_ref.shape[-1]

    def block(kb, carry, mask):
        acc, surv = carry
        start = pl.multiple_of(kb * tq, tq)
        kblk = k_ref[0, pl.ds(start, tq), :].astype(BF16)
        vblk = v_ref[0, pl.ds(start, tq), :].astype(BF16)
        z = lax.dot_general(qb, kblk, (((1,), (1,)), ((), ())),
                            preferred_element_type=F32) * scale + bias
        a, keep_sum = _stick_breaking_weights(z, mask, surv, cum_op, key_axis=1)
        acc = acc + jnp.dot(a.astype(BF16), vblk, preferred_element_type=F32)
        return acc, surv + keep_sum

    ti = lax.broadcasted_iota(jnp.int32, (tq, tq), 0)
    si = lax.broadcasted_iota(jnp.int32, (tq, tq), 1)
    carry = (jnp.zeros((tq, d), F32), jnp.zeros((tq, 1), F32))
    carry = block(qi, carry, si < ti)
    acc, _ = lax.fori_loop(0, qi, lambda i, cr: block(qi - 1 - i, cr, None), carry)
    o_ref[0] = acc.astype(o_ref.dtype)


def _sb_prompt(qkv, bias, n_heads):
    b, s, three_d = qkv.shape
    d_model = three_d // 3
    d = d_model // n_heads
    tq = _largest_tile(s, 256, LANES)
    su = (np.arange(tq)[:, None] > np.arange(tq)[None, :]).astype(np.float32)
    cum_op = jnp.asarray(np.concatenate([su, su], axis=0), BF16)
    return pl.pallas_call(
        functools.partial(_sb_prompt_kernel, tq=tq, scale=d ** -0.5),
        grid=(b, n_heads, s // tq),
        in_specs=[pl.BlockSpec(memory_space=pltpu.SMEM),
                  pl.BlockSpec((1, tq, d), lambda bi, h, qi: (bi, qi, h)),
                  pl.BlockSpec((1, s, d), lambda bi, h, qi: (bi, 0, n_heads + h)),
                  pl.BlockSpec((1, s, d), lambda bi, h, qi: (bi, 0, 2 * n_heads + h)),
                  pl.BlockSpec(cum_op.shape, lambda bi, h, qi: (0, 0))],
        out_specs=pl.BlockSpec((1, tq, d), lambda bi, h, qi: (bi, qi, h)),
        out_shape=jax.ShapeDtypeStruct((b, s, d_model), BF16),
        compiler_params=_params("parallel", "parallel", "arbitrary"),
        name="sb_prompt_attention",
    )(bias.astype(F32), qkv, qkv, qkv, cum_op)


def _sb_sample_kernel(pt_ref, qbd_ref, bias_ref, knew_ref, vnew_ref, kp_ref, vp_ref, cum_ref, o_ref,
                      acc_ref, surv_ref, *, n_queries, n_heads, scale):
    j = pl.program_id(1)
    page = kp_ref.shape[2]
    d = o_ref.shape[-1]
    cols = qbd_ref.shape[-1]

    @pl.when(j == 0)
    def _():
        acc_ref[...] = jnp.zeros_like(acc_ref)
        surv_ref[...] = jnp.zeros_like(surv_ref)

    def process(k, v, mask):
        z = jnp.dot(k.astype(BF16), qbd_ref[0], preferred_element_type=F32) * scale + bias_ref[...]
        a, keep_sum = _stick_breaking_weights(z, mask, surv_ref[...], cum_ref[...], key_axis=0)
        acc_ref[...] += jnp.dot(a.T.astype(BF16), v.astype(BF16), preferred_element_type=F32)
        surv_ref[...] += keep_sum

    @pl.when(j == 0)
    def _():
        si = lax.broadcasted_iota(jnp.int32, (page, cols), 0)
        ti = lax.broadcasted_iota(jnp.int32, (page, cols), 1) % n_queries
        process(knew_ref[0], vnew_ref[0], si < ti)

    @pl.when(j > 0)
    def _():
        process(kp_ref[0, 0], vp_ref[0, 0], None)

    @pl.when(j == pl.num_programs(1) - 1)
    def _():
        row_head = lax.broadcasted_iota(jnp.int32, (cols, d), 0) // n_queries
        res = jnp.zeros((cols, d), F32)
        for h in range(n_heads):
            res = res + jnp.where(row_head == h, acc_ref[:, h * d:(h + 1) * d], 0.0)
        o_ref[0] = res


def _sb_sample(q, k_new, v_new, pool_k, pool_v, layer, page_table, bias):
    db, l, n_heads, d = q.shape
    n_phys, page = pool_k.shape[1:3]
    n_pages = page_table.shape[1]
    d_model = n_heads * d
    cols = n_heads * l
    eye = jnp.eye(n_heads, dtype=F32)
    qbd = jnp.einsum('bthd,hg->bhdgt', q, eye).reshape(db, d_model, cols).astype(BF16)
    bias_row = jnp.repeat(bias.astype(F32), l).reshape(1, cols)
    padk = ((0, 0), (0, page - l), (0, 0))
    knew = jnp.pad(k_new.reshape(db, l, d_model), padk)
    vnew = jnp.pad(v_new.reshape(db, l, d_model), padk)
    kp = pool_k.reshape(pool_k.shape[0], n_phys, page, d_model)
    vp = pool_v.reshape(pool_v.shape[0], n_phys, page, d_model)
    su = (np.arange(page)[None, :] > np.arange(page)[:, None]).astype(np.float32)
    cum_op = jnp.asarray(np.concatenate([su, su], axis=1), BF16)

    def page_map(b, j, pt):
        return (layer, pt[b, n_pages - jnp.maximum(j, 1)], 0, 0)

    out = pl.pallas_call(
        functools.partial(_sb_sample_kernel, n_queries=l, n_heads=n_heads, scale=d ** -0.5),
        grid_spec=pltpu.PrefetchScalarGridSpec(
            num_scalar_prefetch=1,
            grid=(db, n_pages + 1),
            in_specs=[pl.BlockSpec((1, d_model, cols), lambda b, j, pt: (b, 0, 0)),
                      pl.BlockSpec((1, cols), lambda b, j, pt: (0, 0)),
                      pl.BlockSpec((1, page, d_model), lambda b, j, pt: (b, 0, 0)),
                      pl.BlockSpec((1, page, d_model), lambda b, j, pt: (b, 0, 0)),
                      pl.BlockSpec((1, 1, page, d_model), page_map),
                      pl.BlockSpec((1, 1, page, d_model), page_map),
                      pl.BlockSpec(cum_op.shape, lambda b, j, pt: (0, 0))],
            out_specs=pl.BlockSpec((1, cols, d), lambda b, j, pt: (b, 0, 0)),
            scratch_shapes=[pltpu.VMEM((cols, d_model), F32), pltpu.VMEM((1, cols), F32)]),
        out_shape=jax.ShapeDtypeStruct((db, cols, d), F32),
        compiler_params=_params("parallel", "arbitrary"),
        name="sb_sample_attention",
    )(page_table, qbd, bias_row, knew, vnew, kp, vp, cum_op)
    return jnp.transpose(out.reshape(db, n_heads, l, d), (0, 2, 1, 3)).reshape(db, l, d_model)


def _mem_block_kernel(xb_ref, h_ref, wq_ref, mk_ref, mv_ref, wo_ref, g_ref, b_ref, o_ref, ob_ref,
                      *, n_heads, alpha):
    hd = wq_ref.shape[-1] // n_heads
    q = jnp.dot(xb_ref[0], wq_ref[...], preferred_element_type=F32).astype(BF16)
    outs = []
    for hh in range(n_heads):
        cs = slice(hh * hd, (hh + 1) * hd)
        s = lax.dot_general(q[:, cs], mk_ref[0, :, cs], (((1,), (1,)), ((), ())),
                            preferred_element_type=F32) * hd ** -0.5
        e = jnp.exp(s - jnp.max(s, axis=-1, keepdims=True))
        p = e / jnp.sum(e, axis=-1, keepdims=True)
        outs.append(jnp.dot(p.astype(BF16), mv_ref[0, :, cs], preferred_element_type=F32))
    o = jnp.concatenate(outs, axis=1).astype(BF16)
    att = jnp.dot(o, wo_ref[...], preferred_element_type=F32)
    y = _layer_norm_rows(alpha * h_ref[0] + att, g_ref[...], b_ref[...])
    o_ref[0] = y
    ob_ref[0] = y.astype(BF16)


def _mem_block(h, hb, mk, mv, w_q, w_o, g, b, n_heads, alpha):
    bt, l, d = h.shape
    n_mem, hd_all = mk.shape[1:]
    tl = _largest_tile(l, 256, SUBLANES)
    row = pl.BlockSpec((1, tl, d), lambda bi, i: (bi, i, 0))
    mem = pl.BlockSpec((1, n_mem, hd_all), lambda bi, i: (bi, 0, 0))
    vec = pl.BlockSpec((1, d), lambda bi, i: (0, 0))
    return pl.pallas_call(
        functools.partial(_mem_block_kernel, n_heads=n_heads, alpha=alpha),
        grid=(bt, l // tl),
        in_specs=[row, row, pl.BlockSpec(w_q.shape, lambda bi, i: (0, 0)), mem, mem,
                  pl.BlockSpec(w_o.shape, lambda bi, i: (0, 0)), vec, vec],
        out_specs=[row, row],
        out_shape=[jax.ShapeDtypeStruct((bt, l, d), F32), jax.ShapeDtypeStruct((bt, l, d), BF16)],
        compiler_params=_params("parallel", "parallel"),
        name="memory_attention_layernorm",
    )(hb, h, w_q, mk, mv, w_o, g.reshape(1, d), b.reshape(1, d))


def _router_kernel(x_ref, w_ref, o_ref):
    logits = jnp.dot(x_ref[...], w_ref[...], preferred_element_type=F32,
                     precision=lax.Precision.HIGHEST)
    o_ref[...] = _sigmoid(logits)


def _router_scores(x, w):
    t, d = x.shape
    e = w.shape[1]
    tm = _largest_tile(t, 256, SUBLANES)
    return pl.pallas_call(
        _router_kernel,
        grid=(t // tm,),
        in_specs=[pl.BlockSpec((tm, d), lambda i: (i, 0)), pl.BlockSpec((d, e), lambda i: (0, 0))],
        out_specs=pl.BlockSpec((tm, e), lambda i: (i, 0)),
        out_shape=jax.ShapeDtypeStruct((t, e), F32),
        compiler_params=_params("parallel"),
        name="router_scores",
    )(x, w)


def _route(scores, e_bias):
    t, n_experts = scores.shape
    sel = scores + e_bias.astype(F32)
    per_group = n_experts // N_EXPERT_GROUPS
    group_score = jnp.sum(lax.top_k(sel.reshape(t, N_EXPERT_GROUPS, per_group), 2)[0], axis=-1)
    _, top_groups = lax.top_k(group_score, TOPK_GROUPS)
    keep = jnp.repeat(jnp.sum(jax.nn.one_hot(top_groups, N_EXPERT_GROUPS), axis=-2) > 0, per_group, axis=-1)
    _, idx = lax.top_k(jnp.where(keep, sel, -jnp.inf), TOP_K)
    w = jnp.take_along_axis(scores, idx, axis=-1)
    return idx, w / jnp.sum(w, axis=-1, keepdims=True) * ROUTED_SCALE


def _cast_rows(src, dst_ref, rows):
    n = dst_ref.shape[0] // rows

    def body(i, _):
        r = pl.multiple_of(i * rows, rows)
        dst_ref[pl.ds(r, rows), :] = src[pl.ds(r, rows), :].astype(dst_ref.dtype)
        return 0

    lax.fori_loop(0, n, body, 0)


def _expert_changed(be_ref, i):
    return jnp.logical_or(i == 0, be_ref[i] != be_ref[jnp.maximum(i - 1, 0)])


def _moe_up_kernel(be_ref, nu_ref, xs_ref, wg_ref, wu_ref, h_ref, wgb_ref, wub_ref):
    i = pl.program_id(0)

    @pl.when(_expert_changed(be_ref, i))
    def _():
        rows = min(512, wgb_ref.shape[0])
        _cast_rows(wg_ref.at[0, 0], wgb_ref, rows)
        _cast_rows(wu_ref.at[0, 0], wub_ref, rows)

    @pl.when(i < nu_ref[0])
    def _():
        x = xs_ref[...]
        gate = jnp.dot(x, wgb_ref[...], preferred_element_type=F32)
        up = jnp.dot(x, wub_ref[...], preferred_element_type=F32)
        h_ref[...] = (_silu(gate) * up).astype(h_ref.dtype)

    @pl.when(i >= nu_ref[0])
    def _():
        h_ref[...] = jnp.zeros_like(h_ref)


def _moe_down_kernel(be_ref, nu_ref, h_ref, gate_ref, wd_ref, y_ref, wdb_ref):
    i = pl.program_id(0)

    @pl.when(_expert_changed(be_ref, i))
    def _():
        _cast_rows(wd_ref.at[0, 0], wdb_ref, min(128, wdb_ref.shape[0]))

    @pl.when(i < nu_ref[0])
    def _():
        y_ref[...] = jnp.dot(h_ref[...], wdb_ref[...], preferred_element_type=F32) * gate_ref[...]

    @pl.when(i >= nu_ref[0])
    def _():
        y_ref[...] = jnp.zeros_like(y_ref)


def _routed_experts(xb, idx, gate, layer, w_gate, w_up, w_down):
    t, d = xb.shape
    n_experts, _, d_expert = w_gate.shape[1:]
    n = t * TOP_K
    blk = MOE_ROWS
    n_blocks = -(-n // blk) + n_experts
    e_flat = idx.reshape(n)
    order = jnp.argsort(e_flat)
    e_sorted = e_flat[order]
    counts = jnp.bincount(e_flat, length=n_experts)
    padded = (counts + blk - 1) // blk * blk
    ends = jnp.cumsum(padded)
    dest = (ends - padded)[e_sorted] + jnp.arange(n) - (jnp.cumsum(counts) - counts)[e_sorted]
    tok = (order // TOP_K).astype(jnp.int32)
    slot_tok = jnp.zeros((n_blocks * blk,), jnp.int32).at[dest].set(tok)
    slot_gate = jnp.zeros((n_blocks * blk,), F32).at[dest].set(gate.reshape(n)[order])
    block_expert = jnp.minimum(jnp.searchsorted(ends, jnp.arange(n_blocks) * blk, side='right'),
                               n_experts - 1).astype(jnp.int32)
    n_used = (ends[-1] // blk).astype(jnp.int32).reshape(1)
    pos = jnp.zeros((n,), jnp.int32).at[order].set(dest.astype(jnp.int32))

    xs = xb[slot_tok]

    def w_map(i, be, nu):
        return (layer, be[i], 0, 0)

    hmid = pl.pallas_call(
        _moe_up_kernel,
        grid_spec=pltpu.PrefetchScalarGridSpec(
            num_scalar_prefetch=2,
            grid=(n_blocks,),
            in_specs=[pl.BlockSpec((blk, d), lambda i, be, nu: (i, 0)),
                      pl.BlockSpec((1, 1, d, d_expert), w_map),
                      pl.BlockSpec((1, 1, d, d_expert), w_map)],
            out_specs=pl.BlockSpec((blk, d_expert), lambda i, be, nu: (i, 0)),
            scratch_shapes=[pltpu.VMEM((d, d_expert), BF16), pltpu.VMEM((d, d_expert), BF16)]),
        out_shape=jax.ShapeDtypeStruct((n_blocks * blk, d_expert), BF16),
        compiler_params=_params("arbitrary"),
        name="moe_gate_up",
    )(block_expert, n_used, xs, w_gate, w_up)

    y = pl.pallas_call(
        _moe_down_kernel,
        grid_spec=pltpu.PrefetchScalarGridSpec(
            num_scalar_prefetch=2,
            grid=(n_blocks,),
            in_specs=[pl.BlockSpec((blk, d_expert), lambda i, be, nu: (i, 0)),
                      pl.BlockSpec((blk, 1), lambda i, be, nu: (i, 0)),
                      pl.BlockSpec((1, 1, d_expert, d), w_map)],
            out_specs=pl.BlockSpec((blk, d), lambda i, be, nu: (i, 0)),
            scratch_shapes=[pltpu.VMEM((d_expert, d), BF16)]),
        out_shape=jax.ShapeDtypeStruct((n_blocks * blk, d), F32),
        compiler_params=_params("arbitrary"),
        name="moe_down",
    )(block_expert, n_used, hmid, slot_gate.reshape(-1, 1), w_down)

    return jnp.sum(y[pos].reshape(t, TOP_K, d), axis=1)


def _shared_ln_kernel(xb_ref, h_ref, r_ref, sg_ref, su_ref, sd_ref, g_ref, b_ref, o_ref, ob_ref, *, alpha):
    x = xb_ref[...]
    mid = _silu(jnp.dot(x, sg_ref[...], preferred_element_type=F32)) * jnp.dot(
        x, su_ref[...], preferred_element_type=F32)
    shared = jnp.dot(mid.astype(BF16), sd_ref[...], preferred_element_type=F32)
    y = _layer_norm_rows(alpha * h_ref[...] + (r_ref[...] + shared), g_ref[...], b_ref[...])
    o_ref[...] = y
    ob_ref[...] = y.astype(BF16)


def _shared_ln(h, hb, routed, s_gate, s_up, s_down, g, b, alpha):
    t, d = h.shape
    tm = _largest_tile(t, 256, SUBLANES)
    row = pl.BlockSpec((tm, d), lambda i: (i, 0))
    vec = pl.BlockSpec((1, d), lambda i: (0, 0))

    def full(w):
        return pl.BlockSpec(w.shape, lambda i: (0, 0))

    return pl.pallas_call(
        functools.partial(_shared_ln_kernel, alpha=alpha),
        grid=(t // tm,),
        in_specs=[row, row, row, full(s_gate), full(s_up), full(s_down), vec, vec],
        out_specs=[row, row],
        out_shape=[jax.ShapeDtypeStruct((t, d), F32), jax.ShapeDtypeStruct((t, d), BF16)],
        compiler_params=_params("parallel"),
        name="shared_expert_layernorm",
    )(hb, h, routed, s_gate, s_up, s_down, g.reshape(1, d), b.reshape(1, d))


def kernel(x_prompt, x_sample, mem_prompt, cache_ssm_state, cache_conv, cache_sb_k, cache_sb_v, cache_mem_k, cache_mem_v, page_table, ssd_w_in, ssd_conv_w, ssd_conv_b, ssd_dt_bias, ssd_a_log, ssd_d, ssd_norm_g, ssd_w_out, sb_w_qkv, sb_w_o, sb_logit_bias, mem_w_q, mem_w_kv, mem_w_o, router_w, router_bias, moe_w_gate, moe_w_up, moe_w_down, shared_w_gate, shared_w_up, shared_w_down, ln_g, ln_b):
    depth = ln_g.shape[0]
    alpha = (2 * depth) ** 0.25
    b, s, d = x_prompt.shape
    db, dl, _ = x_sample.shape
    n_mem = mem_prompt.shape[1]
    mem_heads = cache_mem_k.shape[3]
    mem_hd = mem_heads * cache_mem_k.shape[4]
    _, _, ssd_heads, ssd_hd, ssd_state = cache_ssm_state.shape
    inner = ssd_heads * ssd_hd
    conv_dim = cache_conv.shape[-1]
    sb_heads, sb_hd = cache_sb_k.shape[3:]
    tp, ts = b * s, db * SAMPLE_ROWS

    hp = x_prompt
    hs = jnp.pad(x_sample, ((0, 0), (0, SAMPLE_ROWS - dl), (0, 0)))
    mem_b = mem_prompt.astype(BF16).reshape(b * n_mem, d)

    ssm_p, conv_p, k_p, v_p, mk_p, mv_p = [], [], [], [], [], []
    ssm_s, conv_s, k_s, v_s = [], [], [], []
    for l in range(depth):
        j = l // N_MIXERS
        if l % N_MIXERS == 0:
            w_in = ssd_w_in[j].astype(BF16)
            w = (w_in[:, :inner], w_in[:, inner:inner + conv_dim], w_in[:, inner + conv_dim:],
                 ssd_conv_w[j], ssd_conv_b[j], ssd_dt_bias[j], ssd_a_log[j], ssd_d[j], ssd_norm_g[j],
                 ssd_w_out[j].astype(BF16))
            mix_p, conv_new, ssm_new = _ssd_mixer(
                hp, s, jnp.zeros((b, cache_conv.shape[2], conv_dim), F32),
                jnp.zeros((b, ssd_heads, ssd_hd, ssd_state), F32), *w)
            conv_p.append(conv_new)
            ssm_p.append(ssm_new)
            mix_s, conv_new, ssm_new = _ssd_mixer(hs, dl, cache_conv[j], cache_ssm_state[j], *w)
            conv_s.append(conv_new)
            ssm_s.append(ssm_new)
        else:
            w_qkv = sb_w_qkv[j].astype(BF16)
            w_o = sb_w_o[j].astype(BF16)
            qkv = _matmul(hp.astype(BF16).reshape(tp, d), w_qkv).reshape(b, s, 3 * d)
            att = _sb_prompt(qkv, sb_logit_bias[j], sb_heads)
            mix_p = _matmul(att.reshape(tp, d), w_o).reshape(b, s, d)
            qkv5 = qkv.reshape(b, s, 3, sb_heads, sb_hd)
            k_p.append(qkv5[:, :, 1])
            v_p.append(qkv5[:, :, 2])

            qkv_s = _matmul(hs.astype(BF16).reshape(ts, d), w_qkv)
            qkv_s = qkv_s.reshape(db, SAMPLE_ROWS, 3, sb_heads, sb_hd)[:, :dl]
            att_s = _sb_sample(qkv_s[:, :, 0], qkv_s[:, :, 1], qkv_s[:, :, 2], cache_sb_k, cache_sb_v,
                               j, page_table, sb_logit_bias[j])
            att_s = jnp.pad(att_s, ((0, 0), (0, SAMPLE_ROWS - dl), (0, 0))).astype(BF16)
            mix_s = _matmul(att_s.reshape(ts, d), w_o).reshape(db, SAMPLE_ROWS, d)
            k_s.append(qkv_s[:, :, 1])
            v_s.append(qkv_s[:, :, 2])

        kv = _matmul(mem_b, mem_w_kv[l].astype(BF16)).reshape(b, n_mem, 2, mem_hd)
        mk, mv = kv[:, :, 0], kv[:, :, 1]
        mk_p.append(mk.reshape(b, n_mem, mem_heads, -1))
        mv_p.append(mv.reshape(b, n_mem, mem_heads, -1))

        hp1, hp1b = _res_ln(hp.reshape(tp, d), mix_p.reshape(tp, d), ln_g[l, 0], ln_b[l, 0], alpha)
        hs1, hs1b = _res_ln(hs.reshape(ts, d), mix_s.reshape(ts, d), ln_g[l, 0], ln_b[l, 0], alpha)
        w_q = mem_w_q[l].astype(BF16)
        w_mo = mem_w_o[l].astype(BF16)
        hp2, hp2b = _mem_block(hp1.reshape(b, s, d), hp1b.reshape(b, s, d), mk.astype(BF16),
                               mv.astype(BF16), w_q, w_mo, ln_g[l, 1], ln_b[l, 1], mem_heads, alpha)
        hs2, hs2b = _mem_block(hs1.reshape(db, SAMPLE_ROWS, d), hs1b.reshape(db, SAMPLE_ROWS, d),
                               cache_mem_k[l].reshape(db, n_mem, mem_hd).astype(BF16),
                               cache_mem_v[l].reshape(db, n_mem, mem_hd).astype(BF16),
                               w_q, w_mo, ln_g[l, 1], ln_b[l, 1], mem_heads, alpha)

        x2 = jnp.concatenate([hp2.reshape(tp, d), hs2.reshape(ts, d)], axis=0)
        x2b = jnp.concatenate([hp2b.reshape(tp, d), hs2b.reshape(ts, d)], axis=0)
        idx, gate = _route(_router_scores(x2, router_w[l]), router_bias[l])
        routed = _routed_experts(x2b, idx, gate, l, moe_w_gate, moe_w_up, moe_w_down)
        h3, _ = _shared_ln(x2, x2b, routed, shared_w_gate[l].astype(BF16), shared_w_up[l].astype(BF16),
                           shared_w_down[l].astype(BF16), ln_g[l, 2], ln_b[l, 2], alpha)
        hp = h3[:tp].reshape(b, s, d)
        hs = h3[tp:].reshape(db, SAMPLE_ROWS, d)

    return (hp, hs[:, :dl], jnp.stack(ssm_p), jnp.stack(conv_p), jnp.stack(k_p), jnp.stack(v_p),
            jnp.stack(mk_p), jnp.stack(mv_p), jnp.stack(ssm_s), jnp.stack(conv_s), jnp.stack(k_s),
            jnp.stack(v_s))
```

```python
import functools

import jax
import jax.numpy as jnp
import numpy as np
from jax import lax
from jax.experimental import pallas as pl
from jax.experimental.pallas import tpu as pltpu

F32 = jnp.float32
BF16 = jnp.bfloat16

SSD_CHUNK = 128
TOP_K = 8
N_EXPERT_GROUPS = 8
TOPK_GROUPS = 4
ROUTED_SCALE = 2.5
N_MIXERS = 2
LN_EPS = 1e-5
RMS_EPS = 1e-5

SUBLANES = 8
LANES = 128
VMEM_LIMIT_BYTES = 56 * 1024 * 1024
MATMUL_BLOCK_BYTES = 8 * 1024 * 1024
SAMPLE_ROWS = 8
MOE_ROWS = 128


def _params(*sem):
    return pltpu.CompilerParams(dimension_semantics=sem, vmem_limit_bytes=VMEM_LIMIT_BYTES)


def _largest_tile(n, cap, quantum):
    if n <= cap:
        return n
    t = cap - cap % quantum
    while t > quantum and n % t:
        t -= quantum
    assert n % t == 0, (n, cap, quantum)
    return t


def _sigmoid(x):
    return 1.0 / (1.0 + jnp.exp(-x))


def _silu(x):
    return x * _sigmoid(x)


def _softplus_neg_abs(x):
    return jnp.log1p(jnp.exp(-jnp.abs(x)))


def _split_bf16(x):
    hi = x.astype(BF16)
    lo = (x - hi.astype(F32)).astype(BF16)
    return hi, lo


def _layer_norm_rows(x, g, b):
    mu = jnp.mean(x, axis=-1, keepdims=True)
    xc = x - mu
    var = jnp.mean(xc * xc, axis=-1, keepdims=True)
    return xc * lax.rsqrt(var + LN_EPS) * g + b


def _matmul_kernel(x_ref, w_ref, o_ref):
    o_ref[...] = jnp.dot(x_ref[...], w_ref[...], preferred_element_type=F32).astype(o_ref.dtype)


def _matmul(x, w, out_dtype=F32):
    m, k = x.shape
    n = w.shape[1]
    cap = max(MATMUL_BLOCK_BYTES // (2 * k), LANES)
    tm = _largest_tile(m, cap, SUBLANES)
    tn = _largest_tile(n, cap, LANES)
    return pl.pallas_call(
        _matmul_kernel,
        grid=(m // tm, n // tn),
        in_specs=[pl.BlockSpec((tm, k), lambda i, j: (i, 0)),
                  pl.BlockSpec((k, tn), lambda i, j: (0, j))],
        out_specs=pl.BlockSpec((tm, tn), lambda i, j: (i, j)),
        out_shape=jax.ShapeDtypeStruct((m, n), out_dtype),
        compiler_params=_params("parallel", "arbitrary"),
        name="dense_matmul",
    )(x, w)


def _res_ln_kernel(h_ref, m_ref, g_ref, b_ref, o_ref, ob_ref, *, alpha):
    y = _layer_norm_rows(alpha * h_ref[...] + m_ref[...], g_ref[...], b_ref[...])
    o_ref[...] = y
    ob_ref[...] = y.astype(BF16)


def _res_ln(h, mix, g, b, alpha):
    t, d = h.shape
    tm = _largest_tile(t, 256, SUBLANES)
    row = pl.BlockSpec((tm, d), lambda i: (i, 0))
    vec = pl.BlockSpec((1, d), lambda i: (0, 0))
    return pl.pallas_call(
        functools.partial(_res_ln_kernel, alpha=alpha),
        grid=(t // tm,),
        in_specs=[row, row, vec, vec],
        out_specs=[row, row],
        out_shape=[jax.ShapeDtypeStruct((t, d), F32), jax.ShapeDtypeStruct((t, d), BF16)],
        compiler_params=_params("parallel"),
        name="residual_layernorm",
    )(h, mix, g.reshape(1, d), b.reshape(1, d))


def _ssd_scan_kernel(z_ref, xs_ref, bm_ref, cm_ref, dt_ref,
                     c0x_ref, c0b_ref, c0c_ref, h0_ref,
                     cwx_ref, cwb_ref, cwc_ref, cbx_ref, cbb_ref, cbc_ref,
                     dtb_ref, alog_ref, dexp_ref, ng_ref, su2_ref,
                     y_ref, hout_ref,
                     extx_ref, extb_ref, extc_ref, keepx_ref, keepb_ref, keepc_ref,
                     dtt_ref, dat_ref, yacc_ref,
                     *, q, l_valid, head_dim, heads_per_group):
    g = pl.program_id(1)
    c = pl.program_id(2)
    taps = cwx_ref.shape[0]
    halo = SUBLANES
    n_state = bm_ref.shape[-1]
    gw = xs_ref.shape[-1]
    pair_w = 2 * head_dim

    @pl.when(c == 0)
    def _():
        keepx_ref[...] = c0x_ref[0]
        keepb_ref[...] = c0b_ref[0]
        keepc_ref[...] = c0c_ref[0]
        hout_ref[0] = h0_ref[0]

    def conv_slab(ext_ref, keep_ref, src_ref, cw_ref, cb_ref):
        width = ext_ref.shape[-1]
        ext_ref[0:halo, :] = keep_ref[...]
        ext_ref[halo:halo + q, :] = src_ref[0]
        keep_ref[...] = ext_ref[q:q + halo, :]
        ct = min(width, 2 * LANES)
        for j in range(width // ct):
            cs = slice(j * ct, (j + 1) * ct)
            acc = cb_ref[:, cs] + ext_ref[halo:halo + q, cs] * cw_ref[taps - 1:taps, cs]
            for tap in range(taps - 1):
                lo = halo - (taps - 1) + tap
                acc = acc + ext_ref[lo:lo + q, cs] * cw_ref[tap:tap + 1, cs]
            ext_ref[halo:halo + q, cs] = _silu(acc)

    conv_slab(extx_ref, keepx_ref, xs_ref, cwx_ref, cbx_ref)
    conv_slab(extb_ref, keepb_ref, bm_ref, cwb_ref, cbb_ref)
    conv_slab(extc_ref, keepc_ref, cm_ref, cwc_ref, cbc_ref)

    dtv = dt_ref[0] + dtb_ref[...]
    dtv = jnp.maximum(dtv, 0.0) + _softplus_neg_abs(dtv)
    rows = lax.broadcasted_iota(jnp.int32, dtv.shape, 0) + c * q
    dtv = jnp.where(rows < l_valid, dtv, 0.0)
    dtt = dtv.T
    dtt_ref[...] = dtt
    dat_ref[...] = dtt * (-jnp.exp(alog_ref[...]))

    bg = extb_ref[halo:halo + q, :]
    cg = extc_ref[halo:halo + q, :]
    cb = lax.dot_general(cg.astype(BF16), bg.astype(BF16), (((1,), (1,)), ((), ())),
                         preferred_element_type=F32)
    bgt = bg.T
    li = lax.broadcasted_iota(jnp.int32, (q, q), 0)
    si = lax.broadcasted_iota(jnp.int32, (q, q), 1)
    tri = si <= li
    tri_f = tri.astype(F32)
    first_head = lax.broadcasted_iota(jnp.int32, (1, pair_w), 1) < head_dim
    su2 = su2_ref[...]

    for pp in range(heads_per_group // 2):
        ls = slice(pp * pair_w, (pp + 1) * pair_w)
        xp = extx_ref[halo:halo + q, ls].astype(BF16)
        sp = hout_ref[0, :, ls]
        rhs_y = jnp.concatenate([xp, sp.astype(BF16)], axis=0)
        ys, us, decs = [], [], []
        for hh in range(2):
            h = g * heads_per_group + pp * 2 + hh
            da_row = dat_ref[pl.ds(h, 1), :]
            dt_row = dtt_ref[pl.ds(h, 1), :]
            hi, lo = _split_bf16(tri_f * da_row)
            dfull = jnp.dot(jnp.concatenate([hi, lo], axis=1), su2, preferred_element_type=F32)
            dm = dfull[:, :q]
            cumcol = dfull[:, q:]
            decay = jnp.where(tri, jnp.exp(dm), 0.0)
            to_end = jnp.exp(dm[q - 1:q, :])
            w_intra = (cb * decay * dt_row).astype(BF16)
            c_inter = (cg * jnp.exp(cumcol[:, :n_state])).astype(BF16)
            ys.append(jnp.dot(jnp.concatenate([w_intra, c_inter], axis=1), rhs_y,
                              preferred_element_type=F32))
            us.append(jnp.dot((bgt * (to_end * dt_row)).astype(BF16), xp,
                              preferred_element_type=F32))
            decs.append(jnp.exp(cumcol[q - 1:q, :pair_w]))
        yacc_ref[:, ls] = jnp.where(first_head, ys[0], ys[1])
        hout_ref[0, :, ls] = (sp * jnp.where(first_head, decs[0], decs[1])
                              + jnp.where(first_head, us[0], us[1]))

    yg = yacc_ref[...] + dexp_ref[...] * extx_ref[halo:halo + q, :]
    yg = yg * _silu(z_ref[0])
    ms = jnp.mean(yg * yg, axis=-1, keepdims=True)
    y_ref[0] = (yg * lax.rsqrt(ms + RMS_EPS) * ng_ref[...]).astype(y_ref.dtype)


def _ssd_scan(z, xbc, dtr, conv0, h0, conv_w, conv_b, dt_bias, a_log, d_skip, norm_g, l_valid):
    bt, lp, inner = z.shape
    cd = xbc.shape[-1]
    n_heads = dtr.shape[-1]
    n_state = h0.shape[1]
    head_dim = inner // n_heads
    n_groups = (cd - inner) // (2 * n_state)
    heads_per_group = n_heads // n_groups
    gw = heads_per_group * head_dim
    q = SSD_CHUNK
    nc = lp // q
    assert lp % q == 0 and (nc == 1 or l_valid == lp)
    assert gw % n_state == 0 and inner % n_state == 0 and heads_per_group % 2 == 0
    taps = conv_w.shape[0]
    b_blk = inner // n_state
    c_blk = (inner + n_groups * n_state) // n_state

    su = (np.arange(q)[:, None] > np.arange(q)[None, :]).astype(np.float32)
    su = np.concatenate([su, np.ones((q, max(n_state, 2 * head_dim)), np.float32)], axis=1)
    su2 = jnp.asarray(np.concatenate([su, su], axis=0), BF16)

    d_exp = jnp.repeat(d_skip.astype(F32), head_dim).reshape(1, inner)

    def xs_map(b, g, c):
        return (b, c, g)

    in_specs = [
        pl.BlockSpec((1, q, gw), xs_map),
        pl.BlockSpec((1, q, gw), xs_map),
        pl.BlockSpec((1, q, n_state), lambda b, g, c: (b, c, b_blk + g)),
        pl.BlockSpec((1, q, n_state), lambda b, g, c: (b, c, c_blk + g)),
        pl.BlockSpec((1, q, n_heads), lambda b, g, c: (b, c, 0)),
        pl.BlockSpec((1, SUBLANES, gw), lambda b, g, c: (b, 0, g)),
        pl.BlockSpec((1, SUBLANES, n_state), lambda b, g, c: (b, 0, b_blk + g)),
        pl.BlockSpec((1, SUBLANES, n_state), lambda b, g, c: (b, 0, c_blk + g)),
        pl.BlockSpec((1, n_state, gw), lambda b, g, c: (b, 0, g)),
        pl.BlockSpec((taps, gw), lambda b, g, c: (0, g)),
        pl.BlockSpec((taps, n_state), lambda b, g, c: (0, b_blk + g)),
        pl.BlockSpec((taps, n_state), lambda b, g, c: (0, c_blk + g)),
        pl.BlockSpec((1, gw), lambda b, g, c: (0, g)),
        pl.BlockSpec((1, n_state), lambda b, g, c: (0, b_blk + g)),
        pl.BlockSpec((1, n_state), lambda b, g, c: (0, c_blk + g)),
        pl.BlockSpec((1, n_heads), lambda b, g, c: (0, 0)),
        pl.BlockSpec((n_heads, 1), lambda b, g, c: (0, 0)),
        pl.BlockSpec((1, gw), lambda b, g, c: (0, g)),
        pl.BlockSpec((1, gw), lambda b, g, c: (0, g)),
        pl.BlockSpec(su2.shape, lambda b, g, c: (0, 0)),
    ]
    out_specs = [pl.BlockSpec((1, q, gw), xs_map),
                 pl.BlockSpec((1, n_state, gw), lambda b, g, c: (b, 0, g))]
    scratch = [pltpu.VMEM((SUBLANES + q, gw), F32), pltpu.VMEM((SUBLANES + q, n_state), F32),
               pltpu.VMEM((SUBLANES + q, n_state), F32),
               pltpu.VMEM((SUBLANES, gw), F32), pltpu.VMEM((SUBLANES, n_state), F32),
               pltpu.VMEM((SUBLANES, n_state), F32),
               pltpu.VMEM((n_heads, q), F32), pltpu.VMEM((n_heads, q), F32),
               pltpu.VMEM((q, gw), F32)]
    cb2 = conv_b.reshape(1, cd)
    return pl.pallas_call(
        functools.partial(_ssd_scan_kernel, q=q, l_valid=l_valid, head_dim=head_dim,
                          heads_per_group=heads_per_group),
        grid=(bt, n_groups, nc),
        in_specs=in_specs,
        out_specs=out_specs,
        out_shape=[jax.ShapeDtypeStruct((bt, lp, inner), BF16),
                   jax.ShapeDtypeStruct((bt, n_state, inner), F32)],
        scratch_shapes=scratch,
        compiler_params=_params("parallel", "arbitrary", "arbitrary"),
        name="ssd_scan",
    )(z, xbc, xbc, xbc, dtr, conv0, conv0, conv0, h0, conv_w, conv_w, conv_w, cb2, cb2, cb2,
      dt_bias.reshape(1, n_heads), a_log.reshape(n_heads, 1), d_exp, norm_g.reshape(1, inner), su2)


def _ssd_mixer(x, l_valid, conv_buf, h0, w_z, w_xbc, w_dt, conv_w, conv_b, dt_bias, a_log, d_skip,
               norm_g, w_out):
    bt, l, d = x.shape
    _, n_heads, head_dim, n_state = h0.shape
    inner = n_heads * head_dim
    xb = x.astype(BF16).reshape(bt * l, d)
    z = _matmul(xb, w_z).reshape(bt, l, inner)
    xbc = _matmul(xb, w_xbc).reshape(bt, l, -1)
    dtr = _matmul(xb, w_dt).reshape(bt, l, n_heads)
    new_conv = jnp.concatenate([conv_buf, xbc[:, :l_valid]], axis=1)[:, l_valid:]

    lp = -(-l // SSD_CHUNK) * SSD_CHUNK
    pad = ((0, 0), (0, lp - l), (0, 0))
    conv0 = jnp.pad(conv_buf, ((0, 0), (SUBLANES - conv_buf.shape[1], 0), (0, 0)))
    h0t = jnp.transpose(h0.reshape(bt, inner, n_state), (0, 2, 1))
    y, ht = _ssd_scan(jnp.pad(z, pad), jnp.pad(xbc, pad), jnp.pad(dtr, pad), conv0, h0t,
                      conv_w, conv_b, dt_bias, a_log, d_skip, norm_g, l_valid)
    mix = _matmul(y[:, :l].reshape(bt * l, inner), w_out).reshape(bt, l, d)
    h_final = jnp.transpose(ht, (0, 2, 1)).reshape(bt, n_heads, head_dim, n_state)
    return mix, new_conv, h_final


def _stick_breaking_weights(z, mask, surv, cum_op, key_axis):
    tail = _softplus_neg_abs(z)
    log_beta = jnp.minimum(z, 0.0) - tail
    log_keep = -jnp.maximum(z, 0.0) - tail
    if mask is not None:
        log_keep = jnp.where(mask, log_keep, 0.0)
    hi, lo = _split_bf16(log_keep)
    if key_axis == 1:
        later = jnp.dot(jnp.concatenate([hi, lo], axis=1), cum_op, preferred_element_type=F32)
    else:
        later = jnp.dot(cum_op, jnp.concatenate([hi, lo], axis=0), preferred_element_type=F32)
    a = jnp.exp(log_beta + later + surv)
    if mask is not None:
        a = jnp.where(mask, a, 0.0)
    return a, jnp.sum(log_keep, axis=key_axis, keepdims=True)


def _sb_prompt_kernel(bias_ref, q_ref, k_ref, v_ref, cum_ref, o_ref, *, tq, scale):
    h = pl.program_id(1)
    qi = pl.program_id(2)
    bias = bias_ref[h]
    qb = q_ref[0].astype(BF16)
    cum_op = cum_ref[...]
    d = q_ref.shape[-1]

    n_sub = tq // LANES

    def logits(kb):
        start = pl.multiple_of(kb * tq, tq)
        kblk = k_ref[0, pl.ds(start, tq), :].astype(BF16)
        return lax.dot_general(qb, kblk, (((1,), (1,)), ((), ())),
                               preferred_element_type=F32) * scale + bias

    def block(kb, z, acc, surv, diagonal):
        start = pl.multiple_of(kb * tq, tq)
        vblk = v_ref[0, pl.ds(start, tq), :].astype(BF16)
        parts = []
        for u in range(n_sub):
            zu = z[:, u * LANES:(u + 1) * LANES]
            mask = None
            if diagonal:
                ti = lax.broadcasted_iota(jnp.int32, zu.shape, 0)
                si = lax.broadcasted_iota(jnp.int32, zu.shape, 1) + u * LANES
                mask = si < ti
            tail = _softplus_neg_abs(zu)
            log_beta = jnp.minimum(zu, 0.0) - tail
            log_keep = -jnp.maximum(zu, 0.0) - tail
            if diagonal:
                log_keep = jnp.where(mask, log_keep, 0.0)
            hi, lo = _split_bf16(log_keep)
            later = jnp.dot(jnp.concatenate([hi, lo], axis=1), cum_op, preferred_element_type=F32)
            parts.append((log_beta + later, jnp.sum(log_keep, axis=1, keepdims=True), mask))
        weights = [None] * n_sub
        for u in reversed(range(n_sub)):
            expo, keep_sum, mask = parts[u]
            a = jnp.exp(expo + surv)
            weights[u] = jnp.where(mask, a, 0.0) if diagonal else a
            surv = surv + keep_sum
        a = jnp.concatenate(weights, axis=1).astype(BF16)
        acc = acc + jnp.dot(a, vblk, preferred_element_type=F32)
        return acc, surv

    z_diag = logits(qi)
    z_first = logits(jnp.maximum(qi - 1, 0))
    acc, surv = block(qi, z_diag, jnp.zeros((tq, d), F32), jnp.zeros((tq, 1), F32), True)

    def body(i, carry):
        acc, surv, z = carry
        kb = qi - 1 - i
        z_next = logits(jnp.maximum(kb - 1, 0))
        acc, surv = block(kb, z, acc, surv, False)
        return acc, surv, z_next

    acc, _, _ = lax.fori_loop(0, qi, body, (acc, surv, z_first))
    o_ref[0] = acc.astype(o_ref.dtype)


def _sb_prompt(qkv, bias, n_heads):
    b, s, three_d = qkv.shape
    d_model = three_d // 3
    d = d_model // n_heads
    tq = _largest_tile(s, 256, LANES)
    su = (np.arange(LANES)[:, None] > np.arange(LANES)[None, :]).astype(np.float32)
    cum_op = jnp.asarray(np.concatenate([su, su], axis=0), BF16)
    return pl.pallas_call(
        functools.partial(_sb_prompt_kernel, tq=tq, scale=d ** -0.5),
        grid=(b, n_heads, s // tq),
        in_specs=[pl.BlockSpec(memory_space=pltpu.SMEM),
                  pl.BlockSpec((1, tq, d), lambda bi, h, qi: (bi, qi, h)),
                  pl.BlockSpec((1, s, d), lambda bi, h, qi: (bi, 0, n_heads + h)),
                  pl.BlockSpec((1, s, d), lambda bi, h, qi: (bi, 0, 2 * n_heads + h)),
                  pl.BlockSpec(cum_op.shape, lambda bi, h, qi: (0, 0))],
        out_specs=pl.BlockSpec((1, tq, d), lambda bi, h, qi: (bi, qi, h)),
        out_shape=jax.ShapeDtypeStruct((b, s, d_model), BF16),
        compiler_params=_params("parallel", "parallel", "arbitrary"),
        name="sb_prompt_attention",
    )(bias.astype(F32), qkv, qkv, qkv, cum_op)


def _sb_sample_kernel(pt_ref, qbd_ref, bias_ref, knew_ref, vnew_ref, kp_ref, vp_ref, cum_ref, o_ref,
                      acc_ref, surv_ref, *, n_queries, n_heads, scale):
    j = pl.program_id(1)
    page = cum_ref.shape[0]
    cols, d = o_ref.shape[1:]
    d_model = n_heads * d

    @pl.when(j == 0)
    def _():
        acc_ref[...] = jnp.zeros_like(acc_ref)
        surv_ref[...] = jnp.zeros_like(surv_ref)

    def process(k, v, mask):
        k = k.astype(BF16).reshape(page, d_model)
        v = v.astype(BF16).reshape(page, d_model)
        z = jnp.dot(k, qbd_ref[0], preferred_element_type=F32) * scale + bias_ref[...]
        a, keep_sum = _stick_breaking_weights(z, mask, surv_ref[...], cum_ref[...], key_axis=0)
        acc_ref[...] += jnp.dot(a.T.astype(BF16), v, preferred_element_type=F32)
        surv_ref[...] += keep_sum

    @pl.when(j == 0)
    def _():
        si = lax.broadcasted_iota(jnp.int32, (page, cols), 0)
        ti = lax.broadcasted_iota(jnp.int32, (page, cols), 1) % n_queries
        process(knew_ref[0], vnew_ref[0], si < ti)

    @pl.when(j > 0)
    def _():
        process(kp_ref[0, 0], vp_ref[0, 0], None)

    @pl.when(j == pl.num_programs(1) - 1)
    def _():
        row_head = lax.broadcasted_iota(jnp.int32, (cols, d), 0) // n_queries
        res = jnp.zeros((cols, d), F32)
        for h in range(n_heads):
            res = res + jnp.where(row_head == h, acc_ref[:, h * d:(h + 1) * d], 0.0)
        o_ref[0] = res


def _sb_sample(q, k_new, v_new, pool_k, pool_v, layer, page_table, bias):
    db, l, n_heads, d = q.shape
    n_phys, page = pool_k.shape[1:3]
    n_pages = page_table.shape[1]
    cols = n_heads * l
    d_model = n_heads * d
    eye = jnp.eye(n_heads, dtype=F32)
    qbd = jnp.einsum('bthd,hg->bhdgt', q, eye).reshape(db, d_model, cols).astype(BF16)
    bias_row = jnp.repeat(bias.astype(F32), l).reshape(1, cols)
    padk = ((0, 0), (0, page - l), (0, 0), (0, 0))
    knew = jnp.pad(k_new, padk).reshape(db, page * n_heads, d)
    vnew = jnp.pad(v_new, padk).reshape(db, page * n_heads, d)
    kp = pool_k.reshape(pool_k.shape[0], n_phys, page * n_heads, d)
    vp = pool_v.reshape(pool_v.shape[0], n_phys, page * n_heads, d)
    su = (np.arange(page)[None, :] > np.arange(page)[:, None]).astype(np.float32)
    cum_op = jnp.asarray(np.concatenate([su, su], axis=1), BF16)

    def page_map(b, j, pt):
        return (layer, pt[b, n_pages - jnp.maximum(j, 1)], 0, 0)

    out = pl.pallas_call(
        functools.partial(_sb_sample_kernel, n_queries=l, n_heads=n_heads, scale=d ** -0.5),
        grid_spec=pltpu.PrefetchScalarGridSpec(
            num_scalar_prefetch=1,
            grid=(db, n_pages + 1),
            in_specs=[pl.BlockSpec((1, d_model, cols), lambda b, j, pt: (b, 0, 0)),
                      pl.BlockSpec((1, cols), lambda b, j, pt: (0, 0)),
                      pl.BlockSpec((1, page * n_heads, d), lambda b, j, pt: (b, 0, 0)),
                      pl.BlockSpec((1, page * n_heads, d), lambda b, j, pt: (b, 0, 0)),
                      pl.BlockSpec((1, 1, page * n_heads, d), page_map),
                      pl.BlockSpec((1, 1, page * n_heads, d), page_map),
                      pl.BlockSpec(cum_op.shape, lambda b, j, pt: (0, 0))],
            out_specs=pl.BlockSpec((1, cols, d), lambda b, j, pt: (b, 0, 0)),
            scratch_shapes=[pltpu.VMEM((cols, d_model), F32), pltpu.VMEM((1, cols), F32)]),
        out_shape=jax.ShapeDtypeStruct((db, cols, d), F32),
        compiler_params=_params("parallel", "arbitrary"),
        name="sb_sample_attention",
    )(page_table, qbd, bias_row, knew, vnew, kp, vp, cum_op)
    return jnp.transpose(out.reshape(db, n_heads, l, d), (0, 2, 1, 3)).reshape(db, l, d_model)


def _mem_block_kernel(xb_ref, h_ref, wq_ref, mk_ref, mv_ref, wo_ref, g_ref, b_ref, o_ref, ob_ref,
                      *, n_heads, alpha):
    hd = wq_ref.shape[-1] // n_heads
    q = jnp.dot(xb_ref[0], wq_ref[...], preferred_element_type=F32).astype(BF16)
    outs = []
    for hh in range(n_heads):
        cs = slice(hh * hd, (hh + 1) * hd)
        s = lax.dot_general(q[:, cs], mk_ref[0, :, cs], (((1,), (1,)), ((), ())),
                            preferred_element_type=F32) * hd ** -0.5
        e = jnp.exp(s - jnp.max(s, axis=-1, keepdims=True))
        p = e / jnp.sum(e, axis=-1, keepdims=True)
        outs.append(jnp.dot(p.astype(BF16), mv_ref[0, :, cs], preferred_element_type=F32))
    o = jnp.concatenate(outs, axis=1).astype(BF16)
    att = jnp.dot(o, wo_ref[...], preferred_element_type=F32)
    y = _layer_norm_rows(alpha * h_ref[0] + att, g_ref[...], b_ref[...])
    o_ref[0] = y
    ob_ref[0] = y.astype(BF16)


def _mem_block(h, hb, mk, mv, w_q, w_o, g, b, n_heads, alpha):
    bt, l, d = h.shape
    n_mem, hd_all = mk.shape[1:]
    tl = _largest_tile(l, 256, SUBLANES)
    row = pl.BlockSpec((1, tl, d), lambda bi, i: (bi, i, 0))
    mem = pl.BlockSpec((1, n_mem, hd_all), lambda bi, i: (bi, 0, 0))
    vec = pl.BlockSpec((1, d), lambda bi, i: (0, 0))
    return pl.pallas_call(
        functools.partial(_mem_block_kernel, n_heads=n_heads, alpha=alpha),
        grid=(bt, l // tl),
        in_specs=[row, row, pl.BlockSpec(w_q.shape, lambda bi, i: (0, 0)), mem, mem,
                  pl.BlockSpec(w_o.shape, lambda bi, i: (0, 0)), vec, vec],
        out_specs=[row, row],
        out_shape=[jax.ShapeDtypeStruct((bt, l, d), F32), jax.ShapeDtypeStruct((bt, l, d), BF16)],
        compiler_params=_params("parallel", "parallel"),
        name="memory_attention_layernorm",
    )(hb, h, w_q, mk, mv, w_o, g.reshape(1, d), b.reshape(1, d))


def _router_kernel(x_ref, w_ref, o_ref):
    logits = jnp.dot(x_ref[...], w_ref[...], preferred_element_type=F32,
                     precision=lax.Precision.HIGHEST)
    o_ref[...] = _sigmoid(logits)


def _router_scores(x, w):
    t, d = x.shape
    e = w.shape[1]
    tm = _largest_tile(t, 256, SUBLANES)
    return pl.pallas_call(
        _router_kernel,
        grid=(t // tm,),
        in_specs=[pl.BlockSpec((tm, d), lambda i: (i, 0)), pl.BlockSpec((d, e), lambda i: (0, 0))],
        out_specs=pl.BlockSpec((tm, e), lambda i: (i, 0)),
        out_shape=jax.ShapeDtypeStruct((t, e), F32),
        compiler_params=_params("parallel"),
        name="router_scores",
    )(x, w)


def _route(scores, e_bias):
    t, n_experts = scores.shape
    sel = scores + e_bias.astype(F32)
    per_group = n_experts // N_EXPERT_GROUPS
    group_score = jnp.sum(lax.top_k(sel.reshape(t, N_EXPERT_GROUPS, per_group), 2)[0], axis=-1)
    _, top_groups = lax.top_k(group_score, TOPK_GROUPS)
    keep = jnp.repeat(jnp.sum(jax.nn.one_hot(top_groups, N_EXPERT_GROUPS), axis=-2) > 0, per_group, axis=-1)
    _, idx = lax.top_k(jnp.where(keep, sel, -jnp.inf), TOP_K)
    w = jnp.take_along_axis(scores, idx, axis=-1)
    return idx, w / jnp.sum(w, axis=-1, keepdims=True) * ROUTED_SCALE


def _cast_rows(src, dst_ref, rows):
    n = dst_ref.shape[0] // rows

    def body(i, _):
        r = pl.multiple_of(i * rows, rows)
        dst_ref[pl.ds(r, rows), :] = src[pl.ds(r, rows), :].astype(dst_ref.dtype)
        return 0

    lax.fori_loop(0, n, body, 0)


def _expert_changed(be_ref, i):
    return jnp.logical_or(i == 0, be_ref[i] != be_ref[jnp.maximum(i - 1, 0)])


def _moe_up_kernel(be_ref, nu_ref, xs_ref, wg_ref, wu_ref, h_ref, wgb_ref, wub_ref):
    i = pl.program_id(0)

    @pl.when(_expert_changed(be_ref, i))
    def _():
        rows = min(512, wgb_ref.shape[0])
        _cast_rows(wg_ref.at[0, 0], wgb_ref, rows)
        _cast_rows(wu_ref.at[0, 0], wub_ref, rows)

    @pl.when(i < nu_ref[0])
    def _():
        x = xs_ref[...]
        gate = jnp.dot(x, wgb_ref[...], preferred_element_type=F32)
        up = jnp.dot(x, wub_ref[...], preferred_element_type=F32)
        h_ref[...] = (_silu(gate) * up).astype(h_ref.dtype)

    @pl.when(i >= nu_ref[0])
    def _():
        h_ref[...] = jnp.zeros_like(h_ref)


def _moe_down_kernel(be_ref, nu_ref, h_ref, wd_ref, y_ref, wdb_ref):
    i = pl.program_id(0)

    @pl.when(_expert_changed(be_ref, i))
    def _():
        _cast_rows(wd_ref.at[0, 0], wdb_ref, min(128, wdb_ref.shape[0]))

    @pl.when(i < nu_ref[0])
    def _():
        y_ref[...] = jnp.dot(h_ref[...], wdb_ref[...], preferred_element_type=F32)

    @pl.when(i >= nu_ref[0])
    def _():
        y_ref[...] = jnp.zeros_like(y_ref)


def _moe_slots_kernel(idx_ref, lt_ref, ut_ref, dest_ref, counts_ref, run_ref, start_ref, *, blk):
    p = pl.program_id(0)
    i = pl.program_id(1)
    tm, top_k = idx_ref.shape
    n_experts = run_ref.shape[-1]
    lanes = lax.broadcasted_iota(jnp.int32, (tm, n_experts), 1)
    idx = idx_ref[...]
    picks = [idx[:, k:k + 1] == lanes for k in range(top_k)]
    chosen = jnp.zeros((tm, n_experts), F32)
    for pk in picks:
        chosen = chosen + pk.astype(F32)
    tile_counts = jnp.sum(chosen, axis=0, keepdims=True)

    @pl.when(jnp.logical_and(p == 0, i == 0))
    def _():
        run_ref[...] = jnp.zeros_like(run_ref)

    @pl.when(p == 0)
    def _():
        run_ref[...] += tile_counts

    @pl.when(jnp.logical_and(p == 1, i == 0))
    def _():
        counts = run_ref[...].astype(jnp.int32)
        counts_ref[...] = counts
        padded = ((counts + (blk - 1)) // blk * blk).astype(F32)
        start_ref[...] = jnp.dot(jnp.broadcast_to(padded, (SUBLANES, n_experts)), ut_ref[...],
                                 preferred_element_type=F32, precision=lax.Precision.HIGHEST)[0:1]
        run_ref[...] = jnp.zeros_like(run_ref)

    @pl.when(p == 1)
    def _():
        earlier = jnp.dot(lt_ref[...], chosen.astype(BF16), preferred_element_type=F32)
        slot = earlier + run_ref[...] + start_ref[...]
        out = jnp.zeros((tm, n_experts), F32)
        for k, pk in enumerate(picks):
            mine = jnp.sum(jnp.where(pk, slot, 0.0), axis=1, keepdims=True)
            out = jnp.where(lanes == k, mine, out)
        dest_ref[...] = out.astype(jnp.int32)
        run_ref[...] += tile_counts


def _moe_slots(idx, n_experts, blk):
    t, top_k = idx.shape
    tm = _largest_tile(t, 256, SUBLANES)
    lt = jnp.asarray((np.arange(tm)[:, None] > np.arange(tm)[None, :]).astype(np.float32), BF16)
    ut = jnp.asarray((np.arange(n_experts)[:, None] < np.arange(n_experts)[None, :]).astype(np.float32))
    dest, counts = pl.pallas_call(
        functools.partial(_moe_slots_kernel, blk=blk),
        grid=(2, t // tm),
        in_specs=[pl.BlockSpec((tm, top_k), lambda p, i: (i, 0)),
                  pl.BlockSpec((tm, tm), lambda p, i: (0, 0)),
                  pl.BlockSpec((n_experts, n_experts), lambda p, i: (0, 0))],
        out_specs=[pl.BlockSpec((tm, n_experts), lambda p, i: (i * p, 0)),
                   pl.BlockSpec((1, n_experts), lambda p, i: (0, 0))],
        out_shape=[jax.ShapeDtypeStruct((t, n_experts), jnp.int32),
                   jax.ShapeDtypeStruct((1, n_experts), jnp.int32)],
        scratch_shapes=[pltpu.VMEM((1, n_experts), F32), pltpu.VMEM((1, n_experts), F32)],
        compiler_params=_params("arbitrary", "arbitrary"),
        name="moe_slots",
    )(idx.astype(jnp.int32), lt, ut)
    return dest[:, :top_k], counts[0]


def _routed_experts(xb, idx, gate, layer, w_gate, w_up, w_down):
    t, d = xb.shape
    n_experts, _, d_expert = w_gate.shape[1:]
    n = t * TOP_K
    blk = MOE_ROWS
    n_blocks = -(-n // blk) + n_experts
    dest, counts = _moe_slots(idx, n_experts, blk)
    ends = jnp.cumsum((counts + blk - 1) // blk * blk)
    tok = jnp.repeat(jnp.arange(t, dtype=jnp.int32), TOP_K)
    slot_tok = jnp.zeros((n_blocks * blk,), jnp.int32).at[dest.reshape(n)].set(tok)
    block_expert = jnp.minimum(jnp.searchsorted(ends, jnp.arange(n_blocks) * blk, side='right'),
                               n_experts - 1).astype(jnp.int32)
    n_used = (ends[-1] // blk).astype(jnp.int32).reshape(1)

    xs = xb[slot_tok]

    def w_map(i, be, nu):
        return (layer, be[i], 0, 0)

    hmid = pl.pallas_call(
        _moe_up_kernel,
        grid_spec=pltpu.PrefetchScalarGridSpec(
            num_scalar_prefetch=2,
            grid=(n_blocks,),
            in_specs=[pl.BlockSpec((blk, d), lambda i, be, nu: (i, 0)),
                      pl.BlockSpec((1, 1, d, d_expert), w_map),
                      pl.BlockSpec((1, 1, d, d_expert), w_map)],
            out_specs=pl.BlockSpec((blk, d_expert), lambda i, be, nu: (i, 0)),
            scratch_shapes=[pltpu.VMEM((d, d_expert), BF16), pltpu.VMEM((d, d_expert), BF16)]),
        out_shape=jax.ShapeDtypeStruct((n_blocks * blk, d_expert), BF16),
        compiler_params=_params("arbitrary"),
        name="moe_gate_up",
    )(block_expert, n_used, xs, w_gate, w_up)

    y = pl.pallas_call(
        _moe_down_kernel,
        grid_spec=pltpu.PrefetchScalarGridSpec(
            num_scalar_prefetch=2,
            grid=(n_blocks,),
            in_specs=[pl.BlockSpec((blk, d_expert), lambda i, be, nu: (i, 0)),
                      pl.BlockSpec((1, 1, d_expert, d), w_map)],
            out_specs=pl.BlockSpec((blk, d), lambda i, be, nu: (i, 0)),
            scratch_shapes=[pltpu.VMEM((d_expert, d), BF16)]),
        out_shape=jax.ShapeDtypeStruct((n_blocks * blk, d), F32),
        compiler_params=_params("arbitrary"),
        name="moe_down",
    )(block_expert, n_used, hmid, w_down)

    return jnp.sum(y[dest] * gate[:, :, None], axis=1)


def _shared_ln_kernel(xb_ref, h_ref, r_ref, sg_ref, su_ref, sd_ref, g_ref, b_ref, o_ref, ob_ref, *, alpha):
    x = xb_ref[...]
    mid = _silu(jnp.dot(x, sg_ref[...], preferred_element_type=F32)) * jnp.dot(
        x, su_ref[...], preferred_element_type=F32)
    shared = jnp.dot(mid.astype(BF16), sd_ref[...], preferred_element_type=F32)
    y = _layer_norm_rows(alpha * h_ref[...] + (r_ref[...] + shared), g_ref[...], b_ref[...])
    o_ref[...] = y
    ob_ref[...] = y.astype(BF16)


def _shared_ln(h, hb, routed, s_gate, s_up, s_down, g, b, alpha):
    t, d = h.shape
    tm = _largest_tile(t, 256, SUBLANES)
    row = pl.BlockSpec((tm, d), lambda i: (i, 0))
    vec = pl.BlockSpec((1, d), lambda i: (0, 0))

    def full(w):
        return pl.BlockSpec(w.shape, lambda i: (0, 0))

    return pl.pallas_call(
        functools.partial(_shared_ln_kernel, alpha=alpha),
        grid=(t // tm,),
        in_specs=[row, row, row, full(s_gate), full(s_up), full(s_down), vec, vec],
        out_specs=[row, row],
        out_shape=[jax.ShapeDtypeStruct((t, d), F32), jax.ShapeDtypeStruct((t, d), BF16)],
        compiler_params=_params("parallel"),
        name="shared_expert_layernorm",
    )(hb, h, routed, s_gate, s_up, s_down, g.reshape(1, d), b.reshape(1, d))


def kernel(x_prompt, x_sample, mem_prompt, cache_ssm_state, cache_conv, cache_sb_k, cache_sb_v, cache_mem_k, cache_mem_v, page_table, ssd_w_in, ssd_conv_w, ssd_conv_b, ssd_dt_bias, ssd_a_log, ssd_d, ssd_norm_g, ssd_w_out, sb_w_qkv, sb_w_o, sb_logit_bias, mem_w_q, mem_w_kv, mem_w_o, router_w, router_bias, moe_w_gate, moe_w_up, moe_w_down, shared_w_gate, shared_w_up, shared_w_down, ln_g, ln_b):
    depth = ln_g.shape[0]
    alpha = (2 * depth) ** 0.25
    b, s, d = x_prompt.shape
    db, dl, _ = x_sample.shape
    n_mem = mem_prompt.shape[1]
    mem_heads = cache_mem_k.shape[3]
    mem_hd = mem_heads * cache_mem_k.shape[4]
    _, _, ssd_heads, ssd_hd, ssd_state = cache_ssm_state.shape
    inner = ssd_heads * ssd_hd
    conv_dim = cache_conv.shape[-1]
    sb_heads, sb_hd = cache_sb_k.shape[3:]
    tp, ts = b * s, db * SAMPLE_ROWS

    hp = x_prompt
    hs = jnp.pad(x_sample, ((0, 0), (0, SAMPLE_ROWS - dl), (0, 0)))
    mem_b = mem_prompt.astype(BF16).reshape(b * n_mem, d)

    ssm_p, conv_p, k_p, v_p, mk_p, mv_p = [], [], [], [], [], []
    ssm_s, conv_s, k_s, v_s = [], [], [], []
    for l in range(depth):
        j = l // N_MIXERS
        if l % N_MIXERS == 0:
            w_in = ssd_w_in[j].astype(BF16)
            w = (w_in[:, :inner], w_in[:, inner:inner + conv_dim], w_in[:, inner + conv_dim:],
                 ssd_conv_w[j], ssd_conv_b[j], ssd_dt_bias[j], ssd_a_log[j], ssd_d[j], ssd_norm_g[j],
                 ssd_w_out[j].astype(BF16))
            mix_p, conv_new, ssm_new = _ssd_mixer(
                hp, s, jnp.zeros((b, cache_conv.shape[2], conv_dim), F32),
                jnp.zeros((b, ssd_heads, ssd_hd, ssd_state), F32), *w)
            conv_p.append(conv_new)
            ssm_p.append(ssm_new)
            mix_s, conv_new, ssm_new = _ssd_mixer(hs, dl, cache_conv[j], cache_ssm_state[j], *w)
            conv_s.append(conv_new)
            ssm_s.append(ssm_new)
        else:
            w_qkv = sb_w_qkv[j].astype(BF16)
            w_o = sb_w_o[j].astype(BF16)
            qkv = _matmul(hp.astype(BF16).reshape(tp, d), w_qkv).reshape(b, s, 3 * d)
            att = _sb_prompt(qkv, sb_logit_bias[j], sb_heads)
            mix_p = _matmul(att.reshape(tp, d), w_o).reshape(b, s, d)
            qkv5 = qkv.reshape(b, s, 3, sb_heads, sb_hd)
            k_p.append(qkv5[:, :, 1])
            v_p.append(qkv5[:, :, 2])

            qkv_s = _matmul(hs.astype(BF16).reshape(ts, d), w_qkv)
            qkv_s = qkv_s.reshape(db, SAMPLE_ROWS, 3, sb_heads, sb_hd)[:, :dl]
            att_s = _sb_sample(qkv_s[:, :, 0], qkv_s[:, :, 1], qkv_s[:, :, 2], cache_sb_k, cache_sb_v,
                               j, page_table, sb_logit_bias[j])
            att_s = jnp.pad(att_s, ((0, 0), (0, SAMPLE_ROWS - dl), (0, 0))).astype(BF16)
            mix_s = _matmul(att_s.reshape(ts, d), w_o).reshape(db, SAMPLE_ROWS, d)
            k_s.append(qkv_s[:, :, 1])
            v_s.append(qkv_s[:, :, 2])

        kv = _matmul(mem_b, mem_w_kv[l].astype(BF16)).reshape(b, n_mem, 2, mem_hd)
        mk, mv = kv[:, :, 0], kv[:, :, 1]
        mk_p.append(mk.reshape(b, n_mem, mem_heads, -1))
        mv_p.append(mv.reshape(b, n_mem, mem_heads, -1))

        hp1, hp1b = _res_ln(hp.reshape(tp, d), mix_p.reshape(tp, d), ln_g[l, 0], ln_b[l, 0], alpha)
        hs1, hs1b = _res_ln(hs.reshape(ts, d), mix_s.reshape(ts, d), ln_g[l, 0], ln_b[l, 0], alpha)
        w_q = mem_w_q[l].astype(BF16)
        w_mo = mem_w_o[l].astype(BF16)
        hp2, hp2b = _mem_block(hp1.reshape(b, s, d), hp1b.reshape(b, s, d), mk.astype(BF16),
                               mv.astype(BF16), w_q, w_mo, ln_g[l, 1], ln_b[l, 1], mem_heads, alpha)
        hs2, hs2b = _mem_block(hs1.reshape(db, SAMPLE_ROWS, d), hs1b.reshape(db, SAMPLE_ROWS, d),
                               cache_mem_k[l].reshape(db, n_mem, mem_hd).astype(BF16),
                               cache_mem_v[l].reshape(db, n_mem, mem_hd).astype(BF16),
                               w_q, w_mo, ln_g[l, 1], ln_b[l, 1], mem_heads, alpha)

        x2 = jnp.concatenate([hp2.reshape(tp, d), hs2.reshape(ts, d)], axis=0)
        x2b = jnp.concatenate([hp2b.reshape(tp, d), hs2b.reshape(ts, d)], axis=0)
        idx, gate = _route(_router_scores(x2, router_w[l]), router_bias[l])
        routed = _routed_experts(x2b, idx, gate, l, moe_w_gate, moe_w_up, moe_w_down)
        h3, _ = _shared_ln(x2, x2b, routed, shared_w_gate[l].astype(BF16), shared_w_up[l].astype(BF16),
                           shared_w_down[l].astype(BF16), ln_g[l, 2], ln_b[l, 2], alpha)
        hp = h3[:tp].reshape(b, s, d)
        hs = h3[tp:].reshape(db, SAMPLE_ROWS, d)

    return (hp, hs[:, :dl], jnp.stack(ssm_p), jnp.stack(conv_p), jnp.stack(k_p), jnp.stack(v_p),
            jnp.stack(mk_p), jnp.stack(mv_p), jnp.stack(ssm_s), jnp.stack(conv_s), jnp.stack(k_s),
            jnp.stack(v_s))
```

```python
import functools

import jax
import jax.numpy as jnp
import numpy as np
from jax import lax
from jax.experimental import pallas as pl
from jax.experimental.pallas import tpu as pltpu

F32 = jnp.float32
BF16 = jnp.bfloat16

SSD_CHUNK = 128
TOP_K = 8
N_EXPERT_GROUPS = 8
TOPK_GROUPS = 4
ROUTED_SCALE = 2.5
N_MIXERS = 2
LN_EPS = 1e-5
RMS_EPS = 1e-5

SUBLANES = 8
LANES = 128
VMEM_LIMIT_BYTES = 56 * 1024 * 1024
MATMUL_BLOCK_BYTES = 8 * 1024 * 1024
SAMPLE_ROWS = 8
MOE_ROWS = 128


def _params(*sem):
    return pltpu.CompilerParams(dimension_semantics=sem, vmem_limit_bytes=VMEM_LIMIT_BYTES)


def _largest_tile(n, cap, quantum):
    if n <= cap:
        return n
    t = cap - cap % quantum
    while t > quantum and n % t:
        t -= quantum
    assert n % t == 0, (n, cap, quantum)
    return t


def _sigmoid(x):
    return 1.0 / (1.0 + jnp.exp(-x))


def _silu(x):
    return x * _sigmoid(x)


def _softplus_neg_abs(x):
    return jnp.log1p(jnp.exp(-jnp.abs(x)))


def _split_bf16(x):
    hi = x.astype(BF16)
    lo = (x - hi.astype(F32)).astype(BF16)
    return hi, lo


def _layer_norm_rows(x, g, b):
    mu = jnp.mean(x, axis=-1, keepdims=True)
    xc = x - mu
    var = jnp.mean(xc * xc, axis=-1, keepdims=True)
    return xc * lax.rsqrt(var + LN_EPS) * g + b


def _matmul_kernel(x_ref, w_ref, o_ref):
    o_ref[...] = jnp.dot(x_ref[...], w_ref[...], preferred_element_type=F32).astype(o_ref.dtype)


def _matmul(x, w, out_dtype=F32):
    m, k = x.shape
    n = w.shape[1]
    cap = max(MATMUL_BLOCK_BYTES // (2 * k), LANES)
    tm = _largest_tile(m, cap, SUBLANES)
    tn = _largest_tile(n, cap, LANES)
    return pl.pallas_call(
        _matmul_kernel,
        grid=(m // tm, n // tn),
        in_specs=[pl.BlockSpec((tm, k), lambda i, j: (i, 0)),
                  pl.BlockSpec((k, tn), lambda i, j: (0, j))],
        out_specs=pl.BlockSpec((tm, tn), lambda i, j: (i, j)),
        out_shape=jax.ShapeDtypeStruct((m, n), out_dtype),
        compiler_params=_params("parallel", "arbitrary"),
        name="dense_matmul",
    )(x, w)


def _res_ln_kernel(h_ref, m_ref, g_ref, b_ref, o_ref, ob_ref, *, alpha):
    y = _layer_norm_rows(alpha * h_ref[...] + m_ref[...], g_ref[...], b_ref[...])
    o_ref[...] = y
    ob_ref[...] = y.astype(BF16)


def _res_ln(h, mix, g, b, alpha):
    t, d = h.shape
    tm = _largest_tile(t, 256, SUBLANES)
    row = pl.BlockSpec((tm, d), lambda i: (i, 0))
    vec = pl.BlockSpec((1, d), lambda i: (0, 0))
    return pl.pallas_call(
        functools.partial(_res_ln_kernel, alpha=alpha),
        grid=(t // tm,),
        in_specs=[row, row, vec, vec],
        out_specs=[row, row],
        out_shape=[jax.ShapeDtypeStruct((t, d), F32), jax.ShapeDtypeStruct((t, d), BF16)],
        compiler_params=_params("parallel"),
        name="residual_layernorm",
    )(h, mix, g.reshape(1, d), b.reshape(1, d))


def _ssd_scan_kernel(z_ref, xs_ref, bm_ref, cm_ref, dt_ref,
                     c0x_ref, c0b_ref, c0c_ref, h0_ref,
                     cwx_ref, cwb_ref, cwc_ref, cbx_ref, cbb_ref, cbc_ref,
                     dtb_ref, alog_ref, dexp_ref, ng_ref, su2_ref,
                     y_ref, hout_ref,
                     extx_ref, extb_ref, extc_ref, keepx_ref, keepb_ref, keepc_ref,
                     dtt_ref, dat_ref, yacc_ref,
                     *, q, l_valid, head_dim, heads_per_group):
    g = pl.program_id(1)
    c = pl.program_id(2)
    taps = cwx_ref.shape[0]
    halo = SUBLANES
    n_state = bm_ref.shape[-1]
    gw = xs_ref.shape[-1]
    pair_w = 2 * head_dim

    @pl.when(c == 0)
    def _():
        keepx_ref[...] = c0x_ref[0]
        keepb_ref[...] = c0b_ref[0]
        keepc_ref[...] = c0c_ref[0]
        hout_ref[0] = h0_ref[0]

    def conv_slab(ext_ref, keep_ref, src_ref, cw_ref, cb_ref):
        width = ext_ref.shape[-1]
        ext_ref[0:halo, :] = keep_ref[...]
        ext_ref[halo:halo + q, :] = src_ref[0]
        keep_ref[...] = ext_ref[q:q + halo, :]
        ct = min(width, 2 * LANES)
        for j in range(width // ct):
            cs = slice(j * ct, (j + 1) * ct)
            acc = cb_ref[:, cs] + ext_ref[halo:halo + q, cs] * cw_ref[taps - 1:taps, cs]
            for tap in range(taps - 1):
                lo = halo - (taps - 1) + tap
                acc = acc + ext_ref[lo:lo + q, cs] * cw_ref[tap:tap + 1, cs]
            ext_ref[halo:halo + q, cs] = _silu(acc)

    conv_slab(extx_ref, keepx_ref, xs_ref, cwx_ref, cbx_ref)
    conv_slab(extb_ref, keepb_ref, bm_ref, cwb_ref, cbb_ref)
    conv_slab(extc_ref, keepc_ref, cm_ref, cwc_ref, cbc_ref)

    dtv = dt_ref[0] + dtb_ref[...]
    dtv = jnp.maximum(dtv, 0.0) + _softplus_neg_abs(dtv)
    rows = lax.broadcasted_iota(jnp.int32, dtv.shape, 0) + c * q
    dtv = jnp.where(rows < l_valid, dtv, 0.0)
    dtt = dtv.T
    dtt_ref[...] = dtt
    dat_ref[...] = dtt * (-jnp.exp(alog_ref[...]))

    bg = extb_ref[halo:halo + q, :]
    cg = extc_ref[halo:halo + q, :]
    cb = lax.dot_general(cg.astype(BF16), bg.astype(BF16), (((1,), (1,)), ((), ())),
                         preferred_element_type=F32)
    bgt = bg.T
    li = lax.broadcasted_iota(jnp.int32, (q, q), 0)
    si = lax.broadcasted_iota(jnp.int32, (q, q), 1)
    tri = si <= li
    tri_f = tri.astype(F32)
    first_head = lax.broadcasted_iota(jnp.int32, (1, pair_w), 1) < head_dim
    su2 = su2_ref[...]

    for pp in range(heads_per_group // 2):
        ls = slice(pp * pair_w, (pp + 1) * pair_w)
        xp = extx_ref[halo:halo + q, ls].astype(BF16)
        sp = hout_ref[0, :, ls]
        rhs_y = jnp.concatenate([xp, sp.astype(BF16)], axis=0)
        ys, us, decs = [], [], []
        for hh in range(2):
            h = g * heads_per_group + pp * 2 + hh
            da_row = dat_ref[pl.ds(h, 1), :]
            dt_row = dtt_ref[pl.ds(h, 1), :]
            hi, lo = _split_bf16(tri_f * da_row)
            dfull = jnp.dot(jnp.concatenate([hi, lo], axis=1), su2, preferred_element_type=F32)
            dm = dfull[:, :q]
            cumcol = dfull[:, q:]
            decay = jnp.where(tri, jnp.exp(dm), 0.0)
            to_end = jnp.exp(dm[q - 1:q, :])
            w_intra = (cb * decay * dt_row).astype(BF16)
            c_inter = (cg * jnp.exp(cumcol[:, :n_state])).astype(BF16)
            ys.append(jnp.dot(jnp.concatenate([w_intra, c_inter], axis=1), rhs_y,
                              preferred_element_type=F32))
            us.append(jnp.dot((bgt * (to_end * dt_row)).astype(BF16), xp,
                              preferred_element_type=F32))
            decs.append(jnp.exp(cumcol[q - 1:q, :pair_w]))
        yacc_ref[:, ls] = jnp.where(first_head, ys[0], ys[1])
        hout_ref[0, :, ls] = (sp * jnp.where(first_head, decs[0], decs[1])
                              + jnp.where(first_head, us[0], us[1]))

    yg = yacc_ref[...] + dexp_ref[...] * extx_ref[halo:halo + q, :]
    yg = yg * _silu(z_ref[0])
    ms = jnp.mean(yg * yg, axis=-1, keepdims=True)
    y_ref[0] = (yg * lax.rsqrt(ms + RMS_EPS) * ng_ref[...]).astype(y_ref.dtype)


def _ssd_scan(z, xbc, dtr, conv0, h0, conv_w, conv_b, dt_bias, a_log, d_skip, norm_g, l_valid):
    bt, lp, inner = z.shape
    cd = xbc.shape[-1]
    n_heads = dtr.shape[-1]
    n_state = h0.shape[1]
    head_dim = inner // n_heads
    n_groups = (cd - inner) // (2 * n_state)
    heads_per_group = n_heads // n_groups
    gw = heads_per_group * head_dim
    q = SSD_CHUNK
    nc = lp // q
    assert lp % q == 0 and (nc == 1 or l_valid == lp)
    assert gw % n_state == 0 and inner % n_state == 0 and heads_per_group % 2 == 0
    taps = conv_w.shape[0]
    b_blk = inner // n_state
    c_blk = (inner + n_groups * n_state) // n_state

    su = (np.arange(q)[:, None] > np.arange(q)[None, :]).astype(np.float32)
    su = np.concatenate([su, np.ones((q, max(n_state, 2 * head_dim)), np.float32)], axis=1)
    su2 = jnp.asarray(np.concatenate([su, su], axis=0), BF16)

    d_exp = jnp.repeat(d_skip.astype(F32), head_dim).reshape(1, inner)

    def xs_map(b, g, c):
        return (b, c, g)

    in_specs = [
        pl.BlockSpec((1, q, gw), xs_map),
        pl.BlockSpec((1, q, gw), xs_map),
        pl.BlockSpec((1, q, n_state), lambda b, g, c: (b, c, b_blk + g)),
        pl.BlockSpec((1, q, n_state), lambda b, g, c: (b, c, c_blk + g)),
        pl.BlockSpec((1, q, n_heads), lambda b, g, c: (b, c, 0)),
        pl.BlockSpec((1, SUBLANES, gw), lambda b, g, c: (b, 0, g)),
        pl.BlockSpec((1, SUBLANES, n_state), lambda b, g, c: (b, 0, b_blk + g)),
        pl.BlockSpec((1, SUBLANES, n_state), lambda b, g, c: (b, 0, c_blk + g)),
        pl.BlockSpec((1, n_state, gw), lambda b, g, c: (b, 0, g)),
        pl.BlockSpec((taps, gw), lambda b, g, c: (0, g)),
        pl.BlockSpec((taps, n_state), lambda b, g, c: (0, b_blk + g)),
        pl.BlockSpec((taps, n_state), lambda b, g, c: (0, c_blk + g)),
        pl.BlockSpec((1, gw), lambda b, g, c: (0, g)),
        pl.BlockSpec((1, n_state), lambda b, g, c: (0, b_blk + g)),
        pl.BlockSpec((1, n_state), lambda b, g, c: (0, c_blk + g)),
        pl.BlockSpec((1, n_heads), lambda b, g, c: (0, 0)),
        pl.BlockSpec((n_heads, 1), lambda b, g, c: (0, 0)),
        pl.BlockSpec((1, gw), lambda b, g, c: (0, g)),
        pl.BlockSpec((1, gw), lambda b, g, c: (0, g)),
        pl.BlockSpec(su2.shape, lambda b, g, c: (0, 0)),
    ]
    out_specs = [pl.BlockSpec((1, q, gw), xs_map),
                 pl.BlockSpec((1, n_state, gw), lambda b, g, c: (b, 0, g))]
    scratch = [pltpu.VMEM((SUBLANES + q, gw), F32), pltpu.VMEM((SUBLANES + q, n_state), F32),
               pltpu.VMEM((SUBLANES + q, n_state), F32),
               pltpu.VMEM((SUBLANES, gw), F32), pltpu.VMEM((SUBLANES, n_state), F32),
               pltpu.VMEM((SUBLANES, n_state), F32),
               pltpu.VMEM((n_heads, q), F32), pltpu.VMEM((n_heads, q), F32),
               pltpu.VMEM((q, gw), F32)]
    cb2 = conv_b.reshape(1, cd)
    return pl.pallas_call(
        functools.partial(_ssd_scan_kernel, q=q, l_valid=l_valid, head_dim=head_dim,
                          heads_per_group=heads_per_group),
        grid=(bt, n_groups, nc),
        in_specs=in_specs,
        out_specs=out_specs,
        out_shape=[jax.ShapeDtypeStruct((bt, lp, inner), BF16),
                   jax.ShapeDtypeStruct((bt, n_state, inner), F32)],
        scratch_shapes=scratch,
        compiler_params=_params("parallel", "arbitrary", "arbitrary"),
        name="ssd_scan",
    )(z, xbc, xbc, xbc, dtr, conv0, conv0, conv0, h0, conv_w, conv_w, conv_w, cb2, cb2, cb2,
      dt_bias.reshape(1, n_heads), a_log.reshape(n_heads, 1), d_exp, norm_g.reshape(1, inner), su2)


def _ssd_mixer(x, l_valid, conv_buf, h0, w_z, w_xbc, w_dt, conv_w, conv_b, dt_bias, a_log, d_skip,
               norm_g, w_out):
    bt, l, d = x.shape
    _, n_heads, head_dim, n_state = h0.shape
    inner = n_heads * head_dim
    xb = x.astype(BF16).reshape(bt * l, d)
    z = _matmul(xb, w_z).reshape(bt, l, inner)
    xbc = _matmul(xb, w_xbc).reshape(bt, l, -1)
    dtr = _matmul(xb, w_dt).reshape(bt, l, n_heads)
    new_conv = jnp.concatenate([conv_buf, xbc[:, :l_valid]], axis=1)[:, l_valid:]

    lp = -(-l // SSD_CHUNK) * SSD_CHUNK
    pad = ((0, 0), (0, lp - l), (0, 0))
    conv0 = jnp.pad(conv_buf, ((0, 0), (SUBLANES - conv_buf.shape[1], 0), (0, 0)))
    h0t = jnp.transpose(h0.reshape(bt, inner, n_state), (0, 2, 1))
    y, ht = _ssd_scan(jnp.pad(z, pad), jnp.pad(xbc, pad), jnp.pad(dtr, pad), conv0, h0t,
                      conv_w, conv_b, dt_bias, a_log, d_skip, norm_g, l_valid)
    mix = _matmul(y[:, :l].reshape(bt * l, inner), w_out).reshape(bt, l, d)
    h_final = jnp.transpose(ht, (0, 2, 1)).reshape(bt, n_heads, head_dim, n_state)
    return mix, new_conv, h_final


def _stick_breaking_weights(z, mask, surv, cum_op, key_axis):
    tail = _softplus_neg_abs(z)
    log_beta = jnp.minimum(z, 0.0) - tail
    log_keep = -jnp.maximum(z, 0.0) - tail
    if mask is not None:
        log_keep = jnp.where(mask, log_keep, 0.0)
    hi, lo = _split_bf16(log_keep)
    if key_axis == 1:
        later = jnp.dot(jnp.concatenate([hi, lo], axis=1), cum_op, preferred_element_type=F32)
    else:
        later = jnp.dot(cum_op, jnp.concatenate([hi, lo], axis=0), preferred_element_type=F32)
    a = jnp.exp(log_beta + later + surv)
    if mask is not None:
        a = jnp.where(mask, a, 0.0)
    return a, jnp.sum(log_keep, axis=key_axis, keepdims=True)


def _sb_prompt_kernel(bias_ref, q_ref, k_ref, v_ref, cum_ref, o_ref, *, tq, scale):
    h = pl.program_id(1)
    qi = pl.program_id(2)
    bias = bias_ref[h]
    qb = q_ref[0].astype(BF16)
    cum_op = cum_ref[...]
    d = q_ref.shape[-1]

    n_sub = tq // LANES

    def logits(kb):
        start = pl.multiple_of(kb * tq, tq)
        kblk = k_ref[0, pl.ds(start, tq), :].astype(BF16)
        return lax.dot_general(qb, kblk, (((1,), (1,)), ((), ())),
                               preferred_element_type=F32) * scale + bias

    def block(kb, z, acc, surv, diagonal):
        start = pl.multiple_of(kb * tq, tq)
        vblk = v_ref[0, pl.ds(start, tq), :].astype(BF16)
        parts = []
        for u in range(n_sub):
            zu = z[:, u * LANES:(u + 1) * LANES]
            mask = None
            if diagonal:
                ti = lax.broadcasted_iota(jnp.int32, zu.shape, 0)
                si = lax.broadcasted_iota(jnp.int32, zu.shape, 1) + u * LANES
                mask = si < ti
            tail = _softplus_neg_abs(zu)
            log_beta = jnp.minimum(zu, 0.0) - tail
            log_keep = -jnp.maximum(zu, 0.0) - tail
            if diagonal:
                log_keep = jnp.where(mask, log_keep, 0.0)
            hi, lo = _split_bf16(log_keep)
            later = jnp.dot(jnp.concatenate([hi, lo], axis=1), cum_op, preferred_element_type=F32)
            parts.append((log_beta + later, jnp.sum(log_keep, axis=1, keepdims=True), mask))
        weights = [None] * n_sub
        for u in reversed(range(n_sub)):
            expo, keep_sum, mask = parts[u]
            a = jnp.exp(expo + surv)
            weights[u] = jnp.where(mask, a, 0.0) if diagonal else a
            surv = surv + keep_sum
        a = jnp.concatenate(weights, axis=1).astype(BF16)
        acc = acc + jnp.dot(a, vblk, preferred_element_type=F32)
        return acc, surv

    z_diag = logits(qi)
    z_first = logits(jnp.maximum(qi - 1, 0))
    acc, surv = block(qi, z_diag, jnp.zeros((tq, d), F32), jnp.zeros((tq, 1), F32), True)

    def body(i, carry):
        acc, surv, z = carry
        kb = qi - 1 - i
        z_next = logits(jnp.maximum(kb - 1, 0))
        acc, surv = block(kb, z, acc, surv, False)
        return acc, surv, z_next

    acc, _, _ = lax.fori_loop(0, qi, body, (acc, surv, z_first))
    o_ref[0] = acc.astype(o_ref.dtype)


def _sb_prompt(qkv, bias, n_heads):
    b, s, three_d = qkv.shape
    d_model = three_d // 3
    d = d_model // n_heads
    tq = _largest_tile(s, 256, LANES)
    su = (np.arange(LANES)[:, None] > np.arange(LANES)[None, :]).astype(np.float32)
    cum_op = jnp.asarray(np.concatenate([su, su], axis=0), BF16)
    return pl.pallas_call(
        functools.partial(_sb_prompt_kernel, tq=tq, scale=d ** -0.5),
        grid=(b, n_heads, s // tq),
        in_specs=[pl.BlockSpec(memory_space=pltpu.SMEM),
                  pl.BlockSpec((1, tq, d), lambda bi, h, qi: (bi, qi, h)),
                  pl.BlockSpec((1, s, d), lambda bi, h, qi: (bi, 0, n_heads + h)),
                  pl.BlockSpec((1, s, d), lambda bi, h, qi: (bi, 0, 2 * n_heads + h)),
                  pl.BlockSpec(cum_op.shape, lambda bi, h, qi: (0, 0))],
        out_specs=pl.BlockSpec((1, tq, d), lambda bi, h, qi: (bi, qi, h)),
        out_shape=jax.ShapeDtypeStruct((b, s, d_model), BF16),
        compiler_params=_params("parallel", "parallel", "arbitrary"),
        name="sb_prompt_attention",
    )(bias.astype(F32), qkv, qkv, qkv, cum_op)


def _sb_sample_kernel(pt_ref, qbd_ref, bias_ref, knew_ref, vnew_ref, kp_ref, vp_ref, cum_ref, o_ref,
                      acc_ref, surv_ref, *, n_queries, n_heads, scale):
    j = pl.program_id(1)
    page = cum_ref.shape[0]
    cols, d = o_ref.shape[1:]
    d_model = n_heads * d

    @pl.when(j == 0)
    def _():
        acc_ref[...] = jnp.zeros_like(acc_ref)
        surv_ref[...] = jnp.zeros_like(surv_ref)

    def process(k, v, mask):
        k = k.astype(BF16).reshape(page, d_model)
        v = v.astype(BF16).reshape(page, d_model)
        z = jnp.dot(k, qbd_ref[0], preferred_element_type=F32) * scale + bias_ref[...]
        a, keep_sum = _stick_breaking_weights(z, mask, surv_ref[...], cum_ref[...], key_axis=0)
        acc_ref[...] += jnp.dot(a.T.astype(BF16), v, preferred_element_type=F32)
        surv_ref[...] += keep_sum

    @pl.when(j == 0)
    def _():
        si = lax.broadcasted_iota(jnp.int32, (page, cols), 0)
        ti = lax.broadcasted_iota(jnp.int32, (page, cols), 1) % n_queries
        process(knew_ref[0], vnew_ref[0], si < ti)

    @pl.when(j > 0)
    def _():
        process(kp_ref[0, 0], vp_ref[0, 0], None)

    @pl.when(j == pl.num_programs(1) - 1)
    def _():
        row_head = lax.broadcasted_iota(jnp.int32, (cols, d), 0) // n_queries
        res = jnp.zeros((cols, d), F32)
        for h in range(n_heads):
            res = res + jnp.where(row_head == h, acc_ref[:, h * d:(h + 1) * d], 0.0)
        o_ref[0] = res


def _sb_sample(q, k_new, v_new, pool_k, pool_v, layer, page_table, bias):
    db, l, n_heads, d = q.shape
    n_phys, page = pool_k.shape[1:3]
    n_pages = page_table.shape[1]
    cols = n_heads * l
    d_model = n_heads * d
    eye = jnp.eye(n_heads, dtype=F32)
    qbd = jnp.einsum('bthd,hg->bhdgt', q, eye).reshape(db, d_model, cols).astype(BF16)
    bias_row = jnp.repeat(bias.astype(F32), l).reshape(1, cols)
    padk = ((0, 0), (0, page - l), (0, 0), (0, 0))
    knew = jnp.pad(k_new, padk).reshape(db, page * n_heads, d)
    vnew = jnp.pad(v_new, padk).reshape(db, page * n_heads, d)
    kp = pool_k.reshape(pool_k.shape[0], n_phys, page * n_heads, d)
    vp = pool_v.reshape(pool_v.shape[0], n_phys, page * n_heads, d)
    su = (np.arange(page)[None, :] > np.arange(page)[:, None]).astype(np.float32)
    cum_op = jnp.asarray(np.concatenate([su, su], axis=1), BF16)

    def page_map(b, j, pt):
        return (layer, pt[b, n_pages - jnp.maximum(j, 1)], 0, 0)

    out = pl.pallas_call(
        functools.partial(_sb_sample_kernel, n_queries=l, n_heads=n_heads, scale=d ** -0.5),
        grid_spec=pltpu.PrefetchScalarGridSpec(
            num_scalar_prefetch=1,
            grid=(db, n_pages + 1),
            in_specs=[pl.BlockSpec((1, d_model, cols), lambda b, j, pt: (b, 0, 0)),
                      pl.BlockSpec((1, cols), lambda b, j, pt: (0, 0)),
                      pl.BlockSpec((1, page * n_heads, d), lambda b, j, pt: (b, 0, 0)),
                      pl.BlockSpec((1, page * n_heads, d), lambda b, j, pt: (b, 0, 0)),
                      pl.BlockSpec((1, 1, page * n_heads, d), page_map),
                      pl.BlockSpec((1, 1, page * n_heads, d), page_map),
                      pl.BlockSpec(cum_op.shape, lambda b, j, pt: (0, 0))],
            out_specs=pl.BlockSpec((1, cols, d), lambda b, j, pt: (b, 0, 0)),
            scratch_shapes=[pltpu.VMEM((cols, d_model), F32), pltpu.VMEM((1, cols), F32)]),
        out_shape=jax.ShapeDtypeStruct((db, cols, d), F32),
        compiler_params=_params("parallel", "arbitrary"),
        name="sb_sample_attention",
    )(page_table, qbd, bias_row, knew, vnew, kp, vp, cum_op)
    return jnp.transpose(out.reshape(db, n_heads, l, d), (0, 2, 1, 3)).reshape(db, l, d_model)


def _mem_block_kernel(xb_ref, h_ref, wq_ref, mk_ref, mv_ref, wo_ref, g_ref, b_ref, o_ref, ob_ref,
                      *, n_heads, alpha):
    hd = wq_ref.shape[-1] // n_heads
    q = jnp.dot(xb_ref[0], wq_ref[...], preferred_element_type=F32).astype(BF16)
    outs = []
    for hh in range(n_heads):
        cs = slice(hh * hd, (hh + 1) * hd)
        s = lax.dot_general(q[:, cs], mk_ref[0, :, cs], (((1,), (1,)), ((), ())),
                            preferred_element_type=F32) * hd ** -0.5
        e = jnp.exp(s - jnp.max(s, axis=-1, keepdims=True))
        p = e / jnp.sum(e, axis=-1, keepdims=True)
        outs.append(jnp.dot(p.astype(BF16), mv_ref[0, :, cs], preferred_element_type=F32))
    o = jnp.concatenate(outs, axis=1).astype(BF16)
    att = jnp.dot(o, wo_ref[...], preferred_element_type=F32)
    y = _layer_norm_rows(alpha * h_ref[0] + att, g_ref[...], b_ref[...])
    o_ref[0] = y
    ob_ref[0] = y.astype(BF16)


def _mem_block(h, hb, mk, mv, w_q, w_o, g, b, n_heads, alpha):
    bt, l, d = h.shape
    n_mem, hd_all = mk.shape[1:]
    tl = _largest_tile(l, 256, SUBLANES)
    row = pl.BlockSpec((1, tl, d), lambda bi, i: (bi, i, 0))
    mem = pl.BlockSpec((1, n_mem, hd_all), lambda bi, i: (bi, 0, 0))
    vec = pl.BlockSpec((1, d), lambda bi, i: (0, 0))
    return pl.pallas_call(
        functools.partial(_mem_block_kernel, n_heads=n_heads, alpha=alpha),
        grid=(bt, l // tl),
        in_specs=[row, row, pl.BlockSpec(w_q.shape, lambda bi, i: (0, 0)), mem, mem,
                  pl.BlockSpec(w_o.shape, lambda bi, i: (0, 0)), vec, vec],
        out_specs=[row, row],
        out_shape=[jax.ShapeDtypeStruct((bt, l, d), F32), jax.ShapeDtypeStruct((bt, l, d), BF16)],
        compiler_params=_params("parallel", "parallel"),
        name="memory_attention_layernorm",
    )(hb, h, w_q, mk, mv, w_o, g.reshape(1, d), b.reshape(1, d))


def _router_kernel(x_ref, w_ref, eb_ref, idx_ref, gate_ref, *, per_group):
    logits = jnp.dot(x_ref[...], w_ref[...], preferred_element_type=F32,
                     precision=lax.Precision.HIGHEST)
    scores = _sigmoid(logits)
    sel = scores + eb_ref[...]
    tm, n_experts = sel.shape
    n_groups = n_experts // per_group
    lane = lax.broadcasted_iota(jnp.int32, (tm, n_experts), 1)
    in_group = lane % per_group
    group = lane // per_group
    neg = -jnp.inf

    def group_rotate(v, s):
        return jnp.where(in_group >= s, pltpu.roll(v, s, axis=1),
                         pltpu.roll(v, (n_experts + s - per_group) % n_experts, axis=1))

    def group_all(v, op):
        s = 1
        while s < per_group:
            v = op(v, group_rotate(v, s))
            s *= 2
        return v

    top1 = group_all(sel, jnp.maximum)
    first = group_all(jnp.where(sel == top1, in_group, per_group), jnp.minimum)
    top2 = group_all(jnp.where(in_group == first, neg, sel), jnp.maximum)
    group_score = top1 + top2

    beaten_by = jnp.zeros((tm, n_experts), jnp.int32)
    for r in range(1, n_groups):
        other = pltpu.roll(group_score, r * per_group, axis=1)
        other_group = (group + (n_groups - r)) % n_groups
        beats = jnp.logical_or(other > group_score,
                               jnp.logical_and(other == group_score, other_group < group))
        beaten_by = beaten_by + beats.astype(jnp.int32)
    cand = jnp.where(beaten_by < TOPK_GROUPS, sel, neg)

    idx_out = jnp.zeros((tm, n_experts), jnp.int32)
    w_out = jnp.zeros((tm, n_experts), F32)
    for k in range(TOP_K):
        best = jnp.max(cand, axis=1, keepdims=True)
        pick = jnp.min(jnp.where(cand == best, lane, n_experts), axis=1, keepdims=True)
        hit = lane == pick
        w_k = jnp.sum(jnp.where(hit, scores, 0.0), axis=1, keepdims=True)
        idx_out = jnp.where(lane == k, pick, idx_out)
        w_out = jnp.where(lane == k, w_k, w_out)
        cand = jnp.where(hit, neg, cand)
    idx_ref[...] = idx_out
    gate_ref[...] = w_out / jnp.sum(w_out, axis=1, keepdims=True) * ROUTED_SCALE


def _route(x, w, e_bias):
    t, d = x.shape
    e = w.shape[1]
    tm = _largest_tile(t, 256, SUBLANES)
    idx, gate = pl.pallas_call(
        functools.partial(_router_kernel, per_group=e // N_EXPERT_GROUPS),
        grid=(t // tm,),
        in_specs=[pl.BlockSpec((tm, d), lambda i: (i, 0)), pl.BlockSpec((d, e), lambda i: (0, 0)),
                  pl.BlockSpec((1, e), lambda i: (0, 0))],
        out_specs=[pl.BlockSpec((tm, e), lambda i: (i, 0)), pl.BlockSpec((tm, e), lambda i: (i, 0))],
        out_shape=[jax.ShapeDtypeStruct((t, e), jnp.int32), jax.ShapeDtypeStruct((t, e), F32)],
        compiler_params=_params("parallel"),
        name="router_topk",
    )(x, w, e_bias.astype(F32).reshape(1, e))
    return idx[:, :TOP_K], gate[:, :TOP_K]


def _cast_rows(src, dst_ref, rows):
    n = dst_ref.shape[0] // rows

    def body(i, _):
        r = pl.multiple_of(i * rows, rows)
        dst_ref[pl.ds(r, rows), :] = src[pl.ds(r, rows), :].astype(dst_ref.dtype)
        return 0

    lax.fori_loop(0, n, body, 0)


def _expert_changed(be_ref, i):
    return jnp.logical_or(i == 0, be_ref[i] != be_ref[jnp.maximum(i - 1, 0)])


def _moe_up_kernel(be_ref, nu_ref, st_ref, x_hbm, wg_ref, wu_ref, h_ref, wgb_ref, wub_ref,
                   xbuf_ref, sem_ref):
    i = pl.program_id(0)
    blk = h_ref.shape[0]
    n_used = nu_ref[0]

    def start_gather(block, slot):
        def body(r, _):
            tok = st_ref[block * blk + r]
            pltpu.make_async_copy(x_hbm.at[pl.ds(tok, 1)], xbuf_ref.at[slot, pl.ds(r, 1)],
                                  sem_ref.at[slot]).start()
            return 0

        lax.fori_loop(0, blk, body, 0, unroll=8)

    @pl.when(jnp.logical_and(i == 0, n_used > 0))
    def _():
        start_gather(0, 0)

    @pl.when(i + 1 < n_used)
    def _():
        start_gather(i + 1, (i + 1) % 2)

    @pl.when(_expert_changed(be_ref, i))
    def _():
        rows = min(512, wgb_ref.shape[0])
        _cast_rows(wg_ref.at[0, 0], wgb_ref, rows)
        _cast_rows(wu_ref.at[0, 0], wub_ref, rows)

    @pl.when(i < n_used)
    def _():
        slot = i % 2
        pltpu.make_async_copy(x_hbm.at[pl.ds(0, blk)], xbuf_ref.at[slot], sem_ref.at[slot]).wait()
        x = xbuf_ref[slot].astype(BF16)
        gate = jnp.dot(x, wgb_ref[...], preferred_element_type=F32)
        up = jnp.dot(x, wub_ref[...], preferred_element_type=F32)
        h_ref[...] = (_silu(gate) * up).astype(h_ref.dtype)

    @pl.when(i >= nu_ref[0])
    def _():
        h_ref[...] = jnp.zeros_like(h_ref)


def _moe_down_kernel(be_ref, nu_ref, h_ref, wd_ref, y_ref, wdb_ref):
    i = pl.program_id(0)

    @pl.when(_expert_changed(be_ref, i))
    def _():
        _cast_rows(wd_ref.at[0, 0], wdb_ref, min(128, wdb_ref.shape[0]))

    @pl.when(i < nu_ref[0])
    def _():
        y_ref[...] = jnp.dot(h_ref[...], wdb_ref[...], preferred_element_type=F32)

    @pl.when(i >= nu_ref[0])
    def _():
        y_ref[...] = jnp.zeros_like(y_ref)


def _moe_slots_kernel(idx_ref, lt_ref, ut_ref, dest_ref, counts_ref, run_ref, start_ref, *, blk):
    p = pl.program_id(0)
    i = pl.program_id(1)
    tm, top_k = idx_ref.shape
    n_experts = run_ref.shape[-1]
    lanes = lax.broadcasted_iota(jnp.int32, (tm, n_experts), 1)
    idx = idx_ref[...]
    picks = [idx[:, k:k + 1] == lanes for k in range(top_k)]
    chosen = jnp.zeros((tm, n_experts), F32)
    for pk in picks:
        chosen = chosen + pk.astype(F32)
    tile_counts = jnp.sum(chosen, axis=0, keepdims=True)

    @pl.when(jnp.logical_and(p == 0, i == 0))
    def _():
        run_ref[...] = jnp.zeros_like(run_ref)

    @pl.when(p == 0)
    def _():
        run_ref[...] += tile_counts

    @pl.when(jnp.logical_and(p == 1, i == 0))
    def _():
        counts = run_ref[...].astype(jnp.int32)
        counts_ref[...] = counts
        padded = ((counts + (blk - 1)) // blk * blk).astype(F32)
        start_ref[...] = jnp.dot(jnp.broadcast_to(padded, (SUBLANES, n_experts)), ut_ref[...],
                                 preferred_element_type=F32, precision=lax.Precision.HIGHEST)[0:1]
        run_ref[...] = jnp.zeros_like(run_ref)

    @pl.when(p == 1)
    def _():
        earlier = jnp.dot(lt_ref[...], chosen.astype(BF16), preferred_element_type=F32)
        slot = earlier + run_ref[...] + start_ref[...]
        out = jnp.zeros((tm, n_experts), F32)
        for k, pk in enumerate(picks):
            mine = jnp.sum(jnp.where(pk, slot, 0.0), axis=1, keepdims=True)
            out = jnp.where(lanes == k, mine, out)
        dest_ref[...] = out.astype(jnp.int32)
        run_ref[...] += tile_counts


def _moe_slots(idx, n_experts, blk):
    t, top_k = idx.shape
    tm = _largest_tile(t, 256, SUBLANES)
    lt = jnp.asarray((np.arange(tm)[:, None] > np.arange(tm)[None, :]).astype(np.float32), BF16)
    ut = jnp.asarray((np.arange(n_experts)[:, None] < np.arange(n_experts)[None, :]).astype(np.float32))
    dest, counts = pl.pallas_call(
        functools.partial(_moe_slots_kernel, blk=blk),
        grid=(2, t // tm),
        in_specs=[pl.BlockSpec((tm, top_k), lambda p, i: (i, 0)),
                  pl.BlockSpec((tm, tm), lambda p, i: (0, 0)),
                  pl.BlockSpec((n_experts, n_experts), lambda p, i: (0, 0))],
        out_specs=[pl.BlockSpec((tm, n_experts), lambda p, i: (i * p, 0)),
                   pl.BlockSpec((1, n_experts), lambda p, i: (0, 0))],
        out_shape=[jax.ShapeDtypeStruct((t, n_experts), jnp.int32),
                   jax.ShapeDtypeStruct((1, n_experts), jnp.int32)],
        scratch_shapes=[pltpu.VMEM((1, n_experts), F32), pltpu.VMEM((1, n_experts), F32)],
        compiler_params=_params("arbitrary", "arbitrary"),
        name="moe_slots",
    )(idx.astype(jnp.int32), lt, ut)
    return dest[:, :top_k], counts[0]


def _routed_experts(x, idx, gate, layer, w_gate, w_up, w_down):
    t, d = x.shape
    n_experts, _, d_expert = w_gate.shape[1:]
    n = t * TOP_K
    blk = MOE_ROWS
    n_blocks = -(-n // blk) + n_experts
    dest, counts = _moe_slots(idx, n_experts, blk)
    ends = jnp.cumsum((counts + blk - 1) // blk * blk)
    tok = jnp.repeat(jnp.arange(t, dtype=jnp.int32), TOP_K)
    slot_tok = jnp.zeros((n_blocks * blk,), jnp.int32).at[dest.reshape(n)].set(tok)
    block_expert = jnp.minimum(jnp.searchsorted(ends, jnp.arange(n_blocks) * blk, side='right'),
                               n_experts - 1).astype(jnp.int32)
    n_used = (ends[-1] // blk).astype(jnp.int32).reshape(1)

    def w_map(i, be, *_):
        return (layer, be[i], 0, 0)

    hmid = pl.pallas_call(
        _moe_up_kernel,
        grid_spec=pltpu.PrefetchScalarGridSpec(
            num_scalar_prefetch=3,
            grid=(n_blocks,),
            in_specs=[pl.BlockSpec(memory_space=pl.ANY),
                      pl.BlockSpec((1, 1, d, d_expert), w_map),
                      pl.BlockSpec((1, 1, d, d_expert), w_map)],
            out_specs=pl.BlockSpec((blk, d_expert), lambda i, *_: (i, 0)),
            scratch_shapes=[pltpu.VMEM((d, d_expert), BF16), pltpu.VMEM((d, d_expert), BF16),
                            pltpu.VMEM((2, blk, d), F32), pltpu.SemaphoreType.DMA((2,))]),
        out_shape=jax.ShapeDtypeStruct((n_blocks * blk, d_expert), BF16),
        compiler_params=_params("arbitrary"),
        name="moe_gate_up",
    )(block_expert, n_used, slot_tok, x, w_gate, w_up)

    y = pl.pallas_call(
        _moe_down_kernel,
        grid_spec=pltpu.PrefetchScalarGridSpec(
            num_scalar_prefetch=2,
            grid=(n_blocks,),
            in_specs=[pl.BlockSpec((blk, d_expert), lambda i, be, nu: (i, 0)),
                      pl.BlockSpec((1, 1, d_expert, d), w_map)],
            out_specs=pl.BlockSpec((blk, d), lambda i, be, nu: (i, 0)),
            scratch_shapes=[pltpu.VMEM((d_expert, d), BF16)]),
        out_shape=jax.ShapeDtypeStruct((n_blocks * blk, d), F32),
        compiler_params=_params("arbitrary"),
        name="moe_down",
    )(block_expert, n_used, hmid, w_down)

    return jnp.sum(y[dest] * gate[:, :, None], axis=1)


def _shared_ln_kernel(xb_ref, h_ref, r_ref, sg_ref, su_ref, sd_ref, g_ref, b_ref, o_ref, ob_ref, *, alpha):
    x = xb_ref[...]
    mid = _silu(jnp.dot(x, sg_ref[...], preferred_element_type=F32)) * jnp.dot(
        x, su_ref[...], preferred_element_type=F32)
    shared = jnp.dot(mid.astype(BF16), sd_ref[...], preferred_element_type=F32)
    y = _layer_norm_rows(alpha * h_ref[...] + (r_ref[...] + shared), g_ref[...], b_ref[...])
    o_ref[...] = y
    ob_ref[...] = y.astype(BF16)


def _shared_ln(h, hb, routed, s_gate, s_up, s_down, g, b, alpha):
    t, d = h.shape
    tm = _largest_tile(t, 256, SUBLANES)
    row = pl.BlockSpec((tm, d), lambda i: (i, 0))
    vec = pl.BlockSpec((1, d), lambda i: (0, 0))

    def full(w):
        return pl.BlockSpec(w.shape, lambda i: (0, 0))

    return pl.pallas_call(
        functools.partial(_shared_ln_kernel, alpha=alpha),
        grid=(t // tm,),
        in_specs=[row, row, row, full(s_gate), full(s_up), full(s_down), vec, vec],
        out_specs=[row, row],
        out_shape=[jax.ShapeDtypeStruct((t, d), F32), jax.ShapeDtypeStruct((t, d), BF16)],
        compiler_params=_params("parallel"),
        name="shared_expert_layernorm",
    )(hb, h, routed, s_gate, s_up, s_down, g.reshape(1, d), b.reshape(1, d))


def kernel(x_prompt, x_sample, mem_prompt, cache_ssm_state, cache_conv, cache_sb_k, cache_sb_v, cache_mem_k, cache_mem_v, page_table, ssd_w_in, ssd_conv_w, ssd_conv_b, ssd_dt_bias, ssd_a_log, ssd_d, ssd_norm_g, ssd_w_out, sb_w_qkv, sb_w_o, sb_logit_bias, mem_w_q, mem_w_kv, mem_w_o, router_w, router_bias, moe_w_gate, moe_w_up, moe_w_down, shared_w_gate, shared_w_up, shared_w_down, ln_g, ln_b):
    depth = ln_g.shape[0]
    alpha = (2 * depth) ** 0.25
    b, s, d = x_prompt.shape
    db, dl, _ = x_sample.shape
    n_mem = mem_prompt.shape[1]
    mem_heads = cache_mem_k.shape[3]
    mem_hd = mem_heads * cache_mem_k.shape[4]
    _, _, ssd_heads, ssd_hd, ssd_state = cache_ssm_state.shape
    inner = ssd_heads * ssd_hd
    conv_dim = cache_conv.shape[-1]
    sb_heads, sb_hd = cache_sb_k.shape[3:]
    tp, ts = b * s, db * SAMPLE_ROWS

    hp = x_prompt
    hs = jnp.pad(x_sample, ((0, 0), (0, SAMPLE_ROWS - dl), (0, 0)))
    mem_b = mem_prompt.astype(BF16).reshape(b * n_mem, d)

    ssm_p, conv_p, k_p, v_p, mk_p, mv_p = [], [], [], [], [], []
    ssm_s, conv_s, k_s, v_s = [], [], [], []
    for l in range(depth):
        j = l // N_MIXERS
        if l % N_MIXERS == 0:
            w_in = ssd_w_in[j].astype(BF16)
            w = (w_in[:, :inner], w_in[:, inner:inner + conv_dim], w_in[:, inner + conv_dim:],
                 ssd_conv_w[j], ssd_conv_b[j], ssd_dt_bias[j], ssd_a_log[j], ssd_d[j], ssd_norm_g[j],
                 ssd_w_out[j].astype(BF16))
            mix_p, conv_new, ssm_new = _ssd_mixer(
                hp, s, jnp.zeros((b, cache_conv.shape[2], conv_dim), F32),
                jnp.zeros((b, ssd_heads, ssd_hd, ssd_state), F32), *w)
            conv_p.append(conv_new)
            ssm_p.append(ssm_new)
            mix_s, conv_new, ssm_new = _ssd_mixer(hs, dl, cache_conv[j], cache_ssm_state[j], *w)
            conv_s.append(conv_new)
            ssm_s.append(ssm_new)
        else:
            w_qkv = sb_w_qkv[j].astype(BF16)
            w_o = sb_w_o[j].astype(BF16)
            qkv = _matmul(hp.astype(BF16).reshape(tp, d), w_qkv).reshape(b, s, 3 * d)
            att = _sb_prompt(qkv, sb_logit_bias[j], sb_heads)
            mix_p = _matmul(att.reshape(tp, d), w_o).reshape(b, s, d)
            qkv5 = qkv.reshape(b, s, 3, sb_heads, sb_hd)
            k_p.append(qkv5[:, :, 1])
            v_p.append(qkv5[:, :, 2])

            qkv_s = _matmul(hs.astype(BF16).reshape(ts, d), w_qkv)
            qkv_s = qkv_s.reshape(db, SAMPLE_ROWS, 3, sb_heads, sb_hd)[:, :dl]
            att_s = _sb_sample(qkv_s[:, :, 0], qkv_s[:, :, 1], qkv_s[:, :, 2], cache_sb_k, cache_sb_v,
                               j, page_table, sb_logit_bias[j])
            att_s = jnp.pad(att_s, ((0, 0), (0, SAMPLE_ROWS - dl), (0, 0))).astype(BF16)
            mix_s = _matmul(att_s.reshape(ts, d), w_o).reshape(db, SAMPLE_ROWS, d)
            k_s.append(qkv_s[:, :, 1])
            v_s.append(qkv_s[:, :, 2])

        kv = _matmul(mem_b, mem_w_kv[l].astype(BF16)).reshape(b, n_mem, 2, mem_hd)
        mk, mv = kv[:, :, 0], kv[:, :, 1]
        mk_p.append(mk.reshape(b, n_mem, mem_heads, -1))
        mv_p.append(mv.reshape(b, n_mem, mem_heads, -1))

        hp1, hp1b = _res_ln(hp.reshape(tp, d), mix_p.reshape(tp, d), ln_g[l, 0], ln_b[l, 0], alpha)
        hs1, hs1b = _res_ln(hs.reshape(ts, d), mix_s.reshape(ts, d), ln_g[l, 0], ln_b[l, 0], alpha)
        w_q = mem_w_q[l].astype(BF16)
        w_mo = mem_w_o[l].astype(BF16)
        hp2, hp2b = _mem_block(hp1.reshape(b, s, d), hp1b.reshape(b, s, d), mk.astype(BF16),
                               mv.astype(BF16), w_q, w_mo, ln_g[l, 1], ln_b[l, 1], mem_heads, alpha)
        hs2, hs2b = _mem_block(hs1.reshape(db, SAMPLE_ROWS, d), hs1b.reshape(db, SAMPLE_ROWS, d),
                               cache_mem_k[l].reshape(db, n_mem, mem_hd).astype(BF16),
                               cache_mem_v[l].reshape(db, n_mem, mem_hd).astype(BF16),
                               w_q, w_mo, ln_g[l, 1], ln_b[l, 1], mem_heads, alpha)

        x2 = jnp.concatenate([hp2.reshape(tp, d), hs2.reshape(ts, d)], axis=0)
        x2b = jnp.concatenate([hp2b.reshape(tp, d), hs2b.reshape(ts, d)], axis=0)
        idx, gate = _route(x2, router_w[l], router_bias[l])
        routed = _routed_experts(x2, idx, gate, l, moe_w_gate, moe_w_up, moe_w_down)
        h3, _ = _shared_ln(x2, x2b, routed, shared_w_gate[l].astype(BF16), shared_w_up[l].astype(BF16),
                           shared_w_down[l].astype(BF16), ln_g[l, 2], ln_b[l, 2], alpha)
        hp = h3[:tp].reshape(b, s, d)
        hs = h3[tp:].reshape(db, SAMPLE_ROWS, d)

    return (hp, hs[:, :dl], jnp.stack(ssm_p), jnp.stack(conv_p), jnp.stack(k_p), jnp.stack(v_p),
            jnp.stack(mk_p), jnp.stack(mv_p), jnp.stack(ssm_s), jnp.stack(conv_s), jnp.stack(k_s),
            jnp.stack(v_s))
```

```python
import functools

import jax
import jax.numpy as jnp
import numpy as np
from jax import lax
from jax.experimental import pallas as pl
from jax.experimental.pallas import tpu as pltpu

F32 = jnp.float32
BF16 = jnp.bfloat16

SSD_CHUNK = 128
TOP_K = 8
N_EXPERT_GROUPS = 8
TOPK_GROUPS = 4
ROUTED_SCALE = 2.5
N_MIXERS = 2
LN_EPS = 1e-5
RMS_EPS = 1e-5

SUBLANES = 8
LANES = 128
VMEM_LIMIT_BYTES = 56 * 1024 * 1024
MATMUL_BLOCK_BYTES = 8 * 1024 * 1024
SAMPLE_ROWS = 8
MOE_ROWS = 128


def _params(*sem):
    return pltpu.CompilerParams(dimension_semantics=sem, vmem_limit_bytes=VMEM_LIMIT_BYTES)


def _largest_tile(n, cap, quantum):
    if n <= cap:
        return n
    t = cap - cap % quantum
    while t > quantum and n % t:
        t -= quantum
    assert n % t == 0, (n, cap, quantum)
    return t


def _sigmoid(x):
    return 1.0 / (1.0 + jnp.exp(-x))


def _silu(x):
    return x * _sigmoid(x)


def _softplus_neg_abs(x):
    return jnp.log1p(jnp.exp(-jnp.abs(x)))


def _split_bf16(x):
    hi = x.astype(BF16)
    lo = (x - hi.astype(F32)).astype(BF16)
    return hi, lo


def _layer_norm_rows(x, g, b):
    mu = jnp.mean(x, axis=-1, keepdims=True)
    xc = x - mu
    var = jnp.mean(xc * xc, axis=-1, keepdims=True)
    return xc * lax.rsqrt(var + LN_EPS) * g + b


def _matmul_kernel(x_ref, w_ref, o_ref):
    o_ref[...] = jnp.dot(x_ref[...], w_ref[...], preferred_element_type=F32).astype(o_ref.dtype)


def _matmul(x, w, out_dtype=F32):
    m, k = x.shape
    n = w.shape[1]
    cap = max(MATMUL_BLOCK_BYTES // (2 * k), LANES)
    tm = _largest_tile(m, cap, SUBLANES)
    tn = _largest_tile(n, cap, LANES)
    return pl.pallas_call(
        _matmul_kernel,
        grid=(m // tm, n // tn),
        in_specs=[pl.BlockSpec((tm, k), lambda i, j: (i, 0)),
                  pl.BlockSpec((k, tn), lambda i, j: (0, j))],
        out_specs=pl.BlockSpec((tm, tn), lambda i, j: (i, j)),
        out_shape=jax.ShapeDtypeStruct((m, n), out_dtype),
        compiler_params=_params("parallel", "arbitrary"),
        name="dense_matmul",
    )(x, w)


def _res_ln_kernel(h_ref, m_ref, g_ref, b_ref, o_ref, ob_ref, *, alpha):
    y = _layer_norm_rows(alpha * h_ref[...] + m_ref[...], g_ref[...], b_ref[...])
    o_ref[...] = y
    ob_ref[...] = y.astype(BF16)


def _res_ln(h, mix, g, b, alpha):
    t, d = h.shape
    tm = _largest_tile(t, 256, SUBLANES)
    row = pl.BlockSpec((tm, d), lambda i: (i, 0))
    vec = pl.BlockSpec((1, d), lambda i: (0, 0))
    return pl.pallas_call(
        functools.partial(_res_ln_kernel, alpha=alpha),
        grid=(t // tm,),
        in_specs=[row, row, vec, vec],
        out_specs=[row, row],
        out_shape=[jax.ShapeDtypeStruct((t, d), F32), jax.ShapeDtypeStruct((t, d), BF16)],
        compiler_params=_params("parallel"),
        name="residual_layernorm",
    )(h, mix, g.reshape(1, d), b.reshape(1, d))


def _ssd_scan_kernel(z_ref, xs_ref, bm_ref, cm_ref, dt_ref,
                     c0x_ref, c0b_ref, c0c_ref, h0_ref,
                     cwx_ref, cwb_ref, cwc_ref, cbx_ref, cbb_ref, cbc_ref,
                     dtb_ref, alog_ref, dexp_ref, ng_ref, su2_ref,
                     y_ref, hout_ref,
                     extx_ref, extb_ref, extc_ref, keepx_ref, keepb_ref, keepc_ref,
                     dtt_ref, dat_ref, yacc_ref,
                     *, q, l_valid, head_dim, heads_per_group):
    g = pl.program_id(1)
    c = pl.program_id(2)
    taps = cwx_ref.shape[0]
    halo = SUBLANES
    n_state = bm_ref.shape[-1]
    gw = xs_ref.shape[-1]
    pair_w = 2 * head_dim

    @pl.when(c == 0)
    def _():
        keepx_ref[...] = c0x_ref[0]
        keepb_ref[...] = c0b_ref[0]
        keepc_ref[...] = c0c_ref[0]
        hout_ref[0] = h0_ref[0]

    def conv_slab(ext_ref, keep_ref, src_ref, cw_ref, cb_ref):
        width = ext_ref.shape[-1]
        ext_ref[0:halo, :] = keep_ref[...]
        ext_ref[halo:halo + q, :] = src_ref[0]
        keep_ref[...] = ext_ref[q:q + halo, :]
        ct = min(width, 2 * LANES)
        for j in range(width // ct):
            cs = slice(j * ct, (j + 1) * ct)
            acc = cb_ref[:, cs] + ext_ref[halo:halo + q, cs] * cw_ref[taps - 1:taps, cs]
            for tap in range(taps - 1):
                lo = halo - (taps - 1) + tap
                acc = acc + ext_ref[lo:lo + q, cs] * cw_ref[tap:tap + 1, cs]
            ext_ref[halo:halo + q, cs] = _silu(acc)

    conv_slab(extx_ref, keepx_ref, xs_ref, cwx_ref, cbx_ref)
    conv_slab(extb_ref, keepb_ref, bm_ref, cwb_ref, cbb_ref)
    conv_slab(extc_ref, keepc_ref, cm_ref, cwc_ref, cbc_ref)

    dtv = dt_ref[0] + dtb_ref[...]
    dtv = jnp.maximum(dtv, 0.0) + _softplus_neg_abs(dtv)
    rows = lax.broadcasted_iota(jnp.int32, dtv.shape, 0) + c * q
    dtv = jnp.where(rows < l_valid, dtv, 0.0)
    dtt = dtv.T
    dtt_ref[...] = dtt
    dat_ref[...] = dtt * (-jnp.exp(alog_ref[...]))

    bg = extb_ref[halo:halo + q, :]
    cg = extc_ref[halo:halo + q, :]
    cb = lax.dot_general(cg.astype(BF16), bg.astype(BF16), (((1,), (1,)), ((), ())),
                         preferred_element_type=F32)
    bgt = bg.T
    li = lax.broadcasted_iota(jnp.int32, (q, q), 0)
    si = lax.broadcasted_iota(jnp.int32, (q, q), 1)
    tri = si <= li
    tri_f = tri.astype(F32)
    first_head = lax.broadcasted_iota(jnp.int32, (1, pair_w), 1) < head_dim
    su2 = su2_ref[...]

    for pp in range(heads_per_group // 2):
        ls = slice(pp * pair_w, (pp + 1) * pair_w)
        xp = extx_ref[halo:halo + q, ls].astype(BF16)
        sp = hout_ref[0, :, ls]
        rhs_y = jnp.concatenate([xp, sp.astype(BF16)], axis=0)
        ys, us, decs = [], [], []
        for hh in range(2):
            h = g * heads_per_group + pp * 2 + hh
            da_row = dat_ref[pl.ds(h, 1), :]
            dt_row = dtt_ref[pl.ds(h, 1), :]
            hi, lo = _split_bf16(tri_f * da_row)
            dfull = jnp.dot(jnp.concatenate([hi, lo], axis=1), su2, preferred_element_type=F32)
            dm = dfull[:, :q]
            cumcol = dfull[:, q:]
            decay = jnp.where(tri, jnp.exp(dm), 0.0)
            to_end = jnp.exp(dm[q - 1:q, :])
            w_intra = (cb * decay * dt_row).astype(BF16)
            c_inter = (cg * jnp.exp(cumcol[:, :n_state])).astype(BF16)
            ys.append(jnp.dot(jnp.concatenate([w_intra, c_inter], axis=1), rhs_y,
                              preferred_element_type=F32))
            us.append(jnp.dot((bgt * (to_end * dt_row)).astype(BF16), xp,
                              preferred_element_type=F32))
            decs.append(jnp.exp(cumcol[q - 1:q, :pair_w]))
        yacc_ref[:, ls] = jnp.where(first_head, ys[0], ys[1])
        hout_ref[0, :, ls] = (sp * jnp.where(first_head, decs[0], decs[1])
                              + jnp.where(first_head, us[0], us[1]))

    yg = yacc_ref[...] + dexp_ref[...] * extx_ref[halo:halo + q, :]
    yg = yg * _silu(z_ref[0])
    ms = jnp.mean(yg * yg, axis=-1, keepdims=True)
    y_ref[0] = (yg * lax.rsqrt(ms + RMS_EPS) * ng_ref[...]).astype(y_ref.dtype)


def _ssd_scan(z, xbc, dtr, conv0, h0, conv_w, conv_b, dt_bias, a_log, d_skip, norm_g, l_valid):
    bt, lp, inner = z.shape
    cd = xbc.shape[-1]
    n_heads = dtr.shape[-1]
    n_state = h0.shape[1]
    head_dim = inner // n_heads
    n_groups = (cd - inner) // (2 * n_state)
    heads_per_group = n_heads // n_groups
    gw = heads_per_group * head_dim
    q = SSD_CHUNK
    nc = lp // q
    assert lp % q == 0 and (nc == 1 or l_valid == lp)
    assert gw % n_state == 0 and inner % n_state == 0 and heads_per_group % 2 == 0
    taps = conv_w.shape[0]
    b_blk = inner // n_state
    c_blk = (inner + n_groups * n_state) // n_state

    su = (np.arange(q)[:, None] > np.arange(q)[None, :]).astype(np.float32)
    su = np.concatenate([su, np.ones((q, max(n_state, 2 * head_dim)), np.float32)], axis=1)
    su2 = jnp.asarray(np.concatenate([su, su], axis=0), BF16)

    d_exp = jnp.repeat(d_skip.astype(F32), head_dim).reshape(1, inner)

    def xs_map(b, g, c):
        return (b, c, g)

    in_specs = [
        pl.BlockSpec((1, q, gw), xs_map),
        pl.BlockSpec((1, q, gw), xs_map),
        pl.BlockSpec((1, q, n_state), lambda b, g, c: (b, c, b_blk + g)),
        pl.BlockSpec((1, q, n_state), lambda b, g, c: (b, c, c_blk + g)),
        pl.BlockSpec((1, q, n_heads), lambda b, g, c: (b, c, 0)),
        pl.BlockSpec((1, SUBLANES, gw), lambda b, g, c: (b, 0, g)),
        pl.BlockSpec((1, SUBLANES, n_state), lambda b, g, c: (b, 0, b_blk + g)),
        pl.BlockSpec((1, SUBLANES, n_state), lambda b, g, c: (b, 0, c_blk + g)),
        pl.BlockSpec((1, n_state, gw), lambda b, g, c: (b, 0, g)),
        pl.BlockSpec((taps, gw), lambda b, g, c: (0, g)),
        pl.BlockSpec((taps, n_state), lambda b, g, c: (0, b_blk + g)),
        pl.BlockSpec((taps, n_state), lambda b, g, c: (0, c_blk + g)),
        pl.BlockSpec((1, gw), lambda b, g, c: (0, g)),
        pl.BlockSpec((1, n_state), lambda b, g, c: (0, b_blk + g)),
        pl.BlockSpec((1, n_state), lambda b, g, c: (0, c_blk + g)),
        pl.BlockSpec((1, n_heads), lambda b, g, c: (0, 0)),
        pl.BlockSpec((n_heads, 1), lambda b, g, c: (0, 0)),
        pl.BlockSpec((1, gw), lambda b, g, c: (0, g)),
        pl.BlockSpec((1, gw), lambda b, g, c: (0, g)),
        pl.BlockSpec(su2.shape, lambda b, g, c: (0, 0)),
    ]
    out_specs = [pl.BlockSpec((1, q, gw), xs_map),
                 pl.BlockSpec((1, n_state, gw), lambda b, g, c: (b, 0, g))]
    scratch = [pltpu.VMEM((SUBLANES + q, gw), F32), pltpu.VMEM((SUBLANES + q, n_state), F32),
               pltpu.VMEM((SUBLANES + q, n_state), F32),
               pltpu.VMEM((SUBLANES, gw), F32), pltpu.VMEM((SUBLANES, n_state), F32),
               pltpu.VMEM((SUBLANES, n_state), F32),
               pltpu.VMEM((n_heads, q), F32), pltpu.VMEM((n_heads, q), F32),
               pltpu.VMEM((q, gw), F32)]
    cb2 = conv_b.reshape(1, cd)
    return pl.pallas_call(
        functools.partial(_ssd_scan_kernel, q=q, l_valid=l_valid, head_dim=head_dim,
                          heads_per_group=heads_per_group),
        grid=(bt, n_groups, nc),
        in_specs=in_specs,
        out_specs=out_specs,
        out_shape=[jax.ShapeDtypeStruct((bt, lp, inner), BF16),
                   jax.ShapeDtypeStruct((bt, n_state, inner), F32)],
        scratch_shapes=scratch,
        compiler_params=_params("parallel", "arbitrary", "arbitrary"),
        name="ssd_scan",
    )(z, xbc, xbc, xbc, dtr, conv0, conv0, conv0, h0, conv_w, conv_w, conv_w, cb2, cb2, cb2,
      dt_bias.reshape(1, n_heads), a_log.reshape(n_heads, 1), d_exp, norm_g.reshape(1, inner), su2)


def _ssd_mixer(x, l_valid, conv_buf, h0, w_z, w_xbc, w_dt, conv_w, conv_b, dt_bias, a_log, d_skip,
               norm_g, w_out):
    bt, l, d = x.shape
    _, n_heads, head_dim, n_state = h0.shape
    inner = n_heads * head_dim
    xb = x.astype(BF16).reshape(bt * l, d)
    z = _matmul(xb, w_z).reshape(bt, l, inner)
    xbc = _matmul(xb, w_xbc).reshape(bt, l, -1)
    dtr = _matmul(xb, w_dt).reshape(bt, l, n_heads)
    new_conv = jnp.concatenate([conv_buf, xbc[:, :l_valid]], axis=1)[:, l_valid:]

    lp = -(-l // SSD_CHUNK) * SSD_CHUNK
    pad = ((0, 0), (0, lp - l), (0, 0))
    conv0 = jnp.pad(conv_buf, ((0, 0), (SUBLANES - conv_buf.shape[1], 0), (0, 0)))
    h0t = jnp.transpose(h0.reshape(bt, inner, n_state), (0, 2, 1))
    y, ht = _ssd_scan(jnp.pad(z, pad), jnp.pad(xbc, pad), jnp.pad(dtr, pad), conv0, h0t,
                      conv_w, conv_b, dt_bias, a_log, d_skip, norm_g, l_valid)
    mix = _matmul(y[:, :l].reshape(bt * l, inner), w_out).reshape(bt, l, d)
    h_final = jnp.transpose(ht, (0, 2, 1)).reshape(bt, n_heads, head_dim, n_state)
    return mix, new_conv, h_final


def _stick_breaking_weights(z, mask, surv, cum_op, key_axis):
    tail = _softplus_neg_abs(z)
    log_beta = jnp.minimum(z, 0.0) - tail
    log_keep = -jnp.maximum(z, 0.0) - tail
    if mask is not None:
        log_keep = jnp.where(mask, log_keep, 0.0)
    hi, lo = _split_bf16(log_keep)
    if key_axis == 1:
        later = jnp.dot(jnp.concatenate([hi, lo], axis=1), cum_op, preferred_element_type=F32)
    else:
        later = jnp.dot(cum_op, jnp.concatenate([hi, lo], axis=0), preferred_element_type=F32)
    a = jnp.exp(log_beta + later + surv)
    if mask is not None:
        a = jnp.where(mask, a, 0.0)
    return a, jnp.sum(log_keep, axis=key_axis, keepdims=True)


def _sb_prompt_kernel(bias_ref, q_ref, k_ref, v_ref, cum_ref, o_ref, *, tq, scale):
    h = pl.program_id(1)
    qi = pl.program_id(2)
    bias = bias_ref[h]
    qb = q_ref[0].astype(BF16)
    cum_op = cum_ref[...]
    d = q_ref.shape[-1]

    n_sub = tq // LANES

    def logits(kb):
        start = pl.multiple_of(kb * tq, tq)
        kblk = k_ref[0, pl.ds(start, tq), :].astype(BF16)
        return lax.dot_general(qb, kblk, (((1,), (1,)), ((), ())),
                               preferred_element_type=F32) * scale + bias

    def block(kb, z, acc, surv, diagonal):
        start = pl.multiple_of(kb * tq, tq)
        vblk = v_ref[0, pl.ds(start, tq), :].astype(BF16)
        parts = []
        for u in range(n_sub):
            zu = z[:, u * LANES:(u + 1) * LANES]
            mask = None
            if diagonal:
                ti = lax.broadcasted_iota(jnp.int32, zu.shape, 0)
                si = lax.broadcasted_iota(jnp.int32, zu.shape, 1) + u * LANES
                mask = si < ti
            tail = _softplus_neg_abs(zu)
            log_beta = jnp.minimum(zu, 0.0) - tail
            log_keep = -jnp.maximum(zu, 0.0) - tail
            if diagonal:
                log_keep = jnp.where(mask, log_keep, 0.0)
            hi, lo = _split_bf16(log_keep)
            later = jnp.dot(jnp.concatenate([hi, lo], axis=1), cum_op, preferred_element_type=F32)
            parts.append((log_beta + later, jnp.sum(log_keep, axis=1, keepdims=True), mask))
        weights = [None] * n_sub
        for u in reversed(range(n_sub)):
            expo, keep_sum, mask = parts[u]
            a = jnp.exp(expo + surv)
            weights[u] = jnp.where(mask, a, 0.0) if diagonal else a
            surv = surv + keep_sum
        a = jnp.concatenate(weights, axis=1).astype(BF16)
        acc = acc + jnp.dot(a, vblk, preferred_element_type=F32)
        return acc, surv

    z_diag = logits(qi)
    z_first = logits(jnp.maximum(qi - 1, 0))
    acc, surv = block(qi, z_diag, jnp.zeros((tq, d), F32), jnp.zeros((tq, 1), F32), True)

    def body(i, carry):
        acc, surv, z = carry
        kb = qi - 1 - i
        z_next = logits(jnp.maximum(kb - 1, 0))
        acc, surv = block(kb, z, acc, surv, False)
        return acc, surv, z_next

    acc, _, _ = lax.fori_loop(0, qi, body, (acc, surv, z_first))
    o_ref[0] = acc.astype(o_ref.dtype)


def _sb_prompt(qkv, bias, n_heads):
    b, s, three_d = qkv.shape
    d_model = three_d // 3
    d = d_model // n_heads
    tq = _largest_tile(s, 256, LANES)
    su = (np.arange(LANES)[:, None] > np.arange(LANES)[None, :]).astype(np.float32)
    cum_op = jnp.asarray(np.concatenate([su, su], axis=0), BF16)
    return pl.pallas_call(
        functools.partial(_sb_prompt_kernel, tq=tq, scale=d ** -0.5),
        grid=(b, n_heads, s // tq),
        in_specs=[pl.BlockSpec(memory_space=pltpu.SMEM),
                  pl.BlockSpec((1, tq, d), lambda bi, h, qi: (bi, qi, h)),
                  pl.BlockSpec((1, s, d), lambda bi, h, qi: (bi, 0, n_heads + h)),
                  pl.BlockSpec((1, s, d), lambda bi, h, qi: (bi, 0, 2 * n_heads + h)),
                  pl.BlockSpec(cum_op.shape, lambda bi, h, qi: (0, 0))],
        out_specs=pl.BlockSpec((1, tq, d), lambda bi, h, qi: (bi, qi, h)),
        out_shape=jax.ShapeDtypeStruct((b, s, d_model), BF16),
        compiler_params=_params("parallel", "parallel", "arbitrary"),
        name="sb_prompt_attention",
    )(bias.astype(F32), qkv, qkv, qkv, cum_op)


def _sb_sample_kernel(pt_ref, qbd_ref, bias_ref, knew_ref, vnew_ref, kp_ref, vp_ref, cum_ref, o_ref,
                      acc_ref, surv_ref, *, n_queries, n_heads, scale):
    j = pl.program_id(1)
    page = cum_ref.shape[0]
    cols, d = o_ref.shape[1:]
    d_model = n_heads * d

    @pl.when(j == 0)
    def _():
        acc_ref[...] = jnp.zeros_like(acc_ref)
        surv_ref[...] = jnp.zeros_like(surv_ref)

    def process(k, v, mask):
        k = k.astype(BF16).reshape(page, d_model)
        v = v.astype(BF16).reshape(page, d_model)
        z = jnp.dot(k, qbd_ref[0], preferred_element_type=F32) * scale + bias_ref[...]
        a, keep_sum = _stick_breaking_weights(z, mask, surv_ref[...], cum_ref[...], key_axis=0)
        acc_ref[...] += jnp.dot(a.T.astype(BF16), v, preferred_element_type=F32)
        surv_ref[...] += keep_sum

    @pl.when(j == 0)
    def _():
        si = lax.broadcasted_iota(jnp.int32, (page, cols), 0)
        ti = lax.broadcasted_iota(jnp.int32, (page, cols), 1) % n_queries
        process(knew_ref[0], vnew_ref[0], si < ti)

    @pl.when(j > 0)
    def _():
        process(kp_ref[0, 0], vp_ref[0, 0], None)

    @pl.when(j == pl.num_programs(1) - 1)
    def _():
        row_head = lax.broadcasted_iota(jnp.int32, (cols, d), 0) // n_queries
        res = jnp.zeros((cols, d), F32)
        for h in range(n_heads):
            res = res + jnp.where(row_head == h, acc_ref[:, h * d:(h + 1) * d], 0.0)
        o_ref[0] = res


def _sb_sample(q, k_new, v_new, pool_k, pool_v, layer, page_table, bias):
    db, l, n_heads, d = q.shape
    n_phys, page = pool_k.shape[1:3]
    n_pages = page_table.shape[1]
    cols = n_heads * l
    d_model = n_heads * d
    eye = jnp.eye(n_heads, dtype=F32)
    qbd = jnp.einsum('bthd,hg->bhdgt', q, eye).reshape(db, d_model, cols).astype(BF16)
    bias_row = jnp.repeat(bias.astype(F32), l).reshape(1, cols)
    padk = ((0, 0), (0, page - l), (0, 0), (0, 0))
    knew = jnp.pad(k_new, padk).reshape(db, page * n_heads, d)
    vnew = jnp.pad(v_new, padk).reshape(db, page * n_heads, d)
    kp = pool_k.reshape(pool_k.shape[0], n_phys, page * n_heads, d)
    vp = pool_v.reshape(pool_v.shape[0], n_phys, page * n_heads, d)
    su = (np.arange(page)[None, :] > np.arange(page)[:, None]).astype(np.float32)
    cum_op = jnp.asarray(np.concatenate([su, su], axis=1), BF16)

    def page_map(b, j, pt):
        return (layer, pt[b, n_pages - jnp.maximum(j, 1)], 0, 0)

    out = pl.pallas_call(
        functools.partial(_sb_sample_kernel, n_queries=l, n_heads=n_heads, scale=d ** -0.5),
        grid_spec=pltpu.PrefetchScalarGridSpec(
            num_scalar_prefetch=1,
            grid=(db, n_pages + 1),
            in_specs=[pl.BlockSpec((1, d_model, cols), lambda b, j, pt: (b, 0, 0)),
                      pl.BlockSpec((1, cols), lambda b, j, pt: (0, 0)),
                      pl.BlockSpec((1, page * n_heads, d), lambda b, j, pt: (b, 0, 0)),
                      pl.BlockSpec((1, page * n_heads, d), lambda b, j, pt: (b, 0, 0)),
                      pl.BlockSpec((1, 1, page * n_heads, d), page_map),
                      pl.BlockSpec((1, 1, page * n_heads, d), page_map),
                      pl.BlockSpec(cum_op.shape, lambda b, j, pt: (0, 0))],
            out_specs=pl.BlockSpec((1, cols, d), lambda b, j, pt: (b, 0, 0)),
            scratch_shapes=[pltpu.VMEM((cols, d_model), F32), pltpu.VMEM((1, cols), F32)]),
        out_shape=jax.ShapeDtypeStruct((db, cols, d), F32),
        compiler_params=_params("parallel", "arbitrary"),
        name="sb_sample_attention",
    )(page_table, qbd, bias_row, knew, vnew, kp, vp, cum_op)
    return jnp.transpose(out.reshape(db, n_heads, l, d), (0, 2, 1, 3)).reshape(db, l, d_model)


def _mem_block_kernel(xb_ref, h_ref, wq_ref, mk_ref, mv_ref, wo_ref, g_ref, b_ref, o_ref, ob_ref,
                      *, n_heads, alpha):
    hd = wq_ref.shape[-1] // n_heads
    q = jnp.dot(xb_ref[0], wq_ref[...], preferred_element_type=F32).astype(BF16)
    outs = []
    for hh in range(n_heads):
        cs = slice(hh * hd, (hh + 1) * hd)
        s = lax.dot_general(q[:, cs], mk_ref[0, :, cs], (((1,), (1,)), ((), ())),
                            preferred_element_type=F32) * hd ** -0.5
        e = jnp.exp(s - jnp.max(s, axis=-1, keepdims=True))
        p = e / jnp.sum(e, axis=-1, keepdims=True)
        outs.append(jnp.dot(p.astype(BF16), mv_ref[0, :, cs], preferred_element_type=F32))
    o = jnp.concatenate(outs, axis=1).astype(BF16)
    att = jnp.dot(o, wo_ref[...], preferred_element_type=F32)
    y = _layer_norm_rows(alpha * h_ref[0] + att, g_ref[...], b_ref[...])
    o_ref[0] = y
    ob_ref[0] = y.astype(BF16)


def _mem_block(h, hb, mk, mv, w_q, w_o, g, b, n_heads, alpha):
    bt, l, d = h.shape
    n_mem, hd_all = mk.shape[1:]
    tl = _largest_tile(l, 256, SUBLANES)
    row = pl.BlockSpec((1, tl, d), lambda bi, i: (bi, i, 0))
    mem = pl.BlockSpec((1, n_mem, hd_all), lambda bi, i: (bi, 0, 0))
    vec = pl.BlockSpec((1, d), lambda bi, i: (0, 0))
    return pl.pallas_call(
        functools.partial(_mem_block_kernel, n_heads=n_heads, alpha=alpha),
        grid=(bt, l // tl),
        in_specs=[row, row, pl.BlockSpec(w_q.shape, lambda bi, i: (0, 0)), mem, mem,
                  pl.BlockSpec(w_o.shape, lambda bi, i: (0, 0)), vec, vec],
        out_specs=[row, row],
        out_shape=[jax.ShapeDtypeStruct((bt, l, d), F32), jax.ShapeDtypeStruct((bt, l, d), BF16)],
        compiler_params=_params("parallel", "parallel"),
        name="memory_attention_layernorm",
    )(hb, h, w_q, mk, mv, w_o, g.reshape(1, d), b.reshape(1, d))


def _router_kernel(x_ref, w_ref, eb_ref, idx_ref, gate_ref, *, per_group):
    logits = jnp.dot(x_ref[...], w_ref[...], preferred_element_type=F32,
                     precision=lax.Precision.HIGHEST)
    scores = _sigmoid(logits)
    sel = scores + eb_ref[...]
    tm, n_experts = sel.shape
    n_groups = n_experts // per_group
    lane = lax.broadcasted_iota(jnp.int32, (tm, n_experts), 1)
    in_group = lane % per_group
    group = lane // per_group
    neg = -jnp.inf

    def group_rotate(v, s):
        return jnp.where(in_group >= s, pltpu.roll(v, s, axis=1),
                         pltpu.roll(v, (n_experts + s - per_group) % n_experts, axis=1))

    def group_all(v, op):
        s = 1
        while s < per_group:
            v = op(v, group_rotate(v, s))
            s *= 2
        return v

    top1 = group_all(sel, jnp.maximum)
    first = group_all(jnp.where(sel == top1, in_group, per_group), jnp.minimum)
    top2 = group_all(jnp.where(in_group == first, neg, sel), jnp.maximum)
    group_score = top1 + top2

    beaten_by = jnp.zeros((tm, n_experts), jnp.int32)
    for r in range(1, n_groups):
        other = pltpu.roll(group_score, r * per_group, axis=1)
        other_group = (group + (n_groups - r)) % n_groups
        beats = jnp.logical_or(other > group_score,
                               jnp.logical_and(other == group_score, other_group < group))
        beaten_by = beaten_by + beats.astype(jnp.int32)
    cand = jnp.where(beaten_by < TOPK_GROUPS, sel, neg)

    idx_out = jnp.zeros((tm, n_experts), jnp.int32)
    w_out = jnp.zeros((tm, n_experts), F32)
    for k in range(TOP_K):
        best = jnp.max(cand, axis=1, keepdims=True)
        pick = jnp.min(jnp.where(cand == best, lane, n_experts), axis=1, keepdims=True)
        hit = lane == pick
        w_k = jnp.sum(jnp.where(hit, scores, 0.0), axis=1, keepdims=True)
        idx_out = jnp.where(lane == k, pick, idx_out)
        w_out = jnp.where(lane == k, w_k, w_out)
        cand = jnp.where(hit, neg, cand)
    idx_ref[...] = idx_out
    gate_ref[...] = w_out / jnp.sum(w_out, axis=1, keepdims=True) * ROUTED_SCALE


def _route(x, w, e_bias):
    t, d = x.shape
    e = w.shape[1]
    tm = _largest_tile(t, 256, SUBLANES)
    idx, gate = pl.pallas_call(
        functools.partial(_router_kernel, per_group=e // N_EXPERT_GROUPS),
        grid=(t // tm,),
        in_specs=[pl.BlockSpec((tm, d), lambda i: (i, 0)), pl.BlockSpec((d, e), lambda i: (0, 0)),
                  pl.BlockSpec((1, e), lambda i: (0, 0))],
        out_specs=[pl.BlockSpec((tm, e), lambda i: (i, 0)), pl.BlockSpec((tm, e), lambda i: (i, 0))],
        out_shape=[jax.ShapeDtypeStruct((t, e), jnp.int32), jax.ShapeDtypeStruct((t, e), F32)],
        compiler_params=_params("parallel"),
        name="router_topk",
    )(x, w, e_bias.astype(F32).reshape(1, e))
    return idx[:, :TOP_K], gate[:, :TOP_K]


def _cast_rows(src, dst_ref, rows):
    n = dst_ref.shape[0] // rows

    def body(i, _):
        r = pl.multiple_of(i * rows, rows)
        dst_ref[pl.ds(r, rows), :] = src[pl.ds(r, rows), :].astype(dst_ref.dtype)
        return 0

    lax.fori_loop(0, n, body, 0)


def _expert_changed(be_ref, i):
    return jnp.logical_or(i == 0, be_ref[i] != be_ref[jnp.maximum(i - 1, 0)])


GATHER_SLOTS = 3


def _expert_weights(be_ref, nxt_ref, par_ref, layer, weights, sem_ref):
    i = pl.program_id(0)
    e = be_ref[i]
    par = par_ref[i]

    def copies(expert, slot):
        return [pltpu.make_async_copy(w.at[layer, expert], wf.at[slot], sem_ref.at[j, slot])
                for j, (w, wf, _, _) in enumerate(weights)]

    @pl.when(i == 0)
    def _():
        for c in copies(e, par):
            c.start()

    @pl.when(_expert_changed(be_ref, i))
    def _():
        nxt = nxt_ref[i]

        @pl.when(nxt != e)
        def _():
            for c in copies(nxt, 1 - par):
                c.start()

        for c in copies(e, par):
            c.wait()
        for _, wf, wb, rows in weights:
            _cast_rows(wf.at[par], wb, rows)


def _moe_up_kernel(be_ref, nu_ref, st_ref, nxt_ref, par_ref, x_hbm, wg_hbm, wu_hbm, h_ref,
                   wgb_ref, wub_ref, wgf_ref, wuf_ref, xbuf_ref, sem_ref, wsem_ref, *, layer):
    i = pl.program_id(0)
    last = pl.num_programs(0) - 1
    blk = h_ref.shape[0]

    def start_gather(block, slot):
        for r in range(blk):
            tok = st_ref[block * blk + r]
            pltpu.make_async_copy(x_hbm.at[pl.ds(tok, 1)], xbuf_ref.at[slot, pl.ds(r, 1)],
                                  sem_ref.at[slot]).start()

    def wait_gather(slot):
        pltpu.make_async_copy(x_hbm.at[pl.ds(0, blk)], xbuf_ref.at[slot], sem_ref.at[slot]).wait()

    @pl.when(i == 0)
    def _():
        start_gather(0, 0)
        start_gather(jnp.minimum(1, last), 1)

    rows = min(512, wgb_ref.shape[0])
    _expert_weights(be_ref, nxt_ref, par_ref, layer,
                    [(wg_hbm, wgf_ref, wgb_ref, rows), (wu_hbm, wuf_ref, wub_ref, rows)], wsem_ref)

    slot = i % GATHER_SLOTS
    wait_gather(slot)
    x = xbuf_ref[slot].astype(BF16)
    gate = jnp.dot(x, wgb_ref[...], preferred_element_type=F32)
    up = jnp.dot(x, wub_ref[...], preferred_element_type=F32)
    h_ref[...] = (_silu(gate) * up).astype(h_ref.dtype)
    start_gather(jnp.minimum(i + 2, last), (i + 2) % GATHER_SLOTS)

    @pl.when(i == last)
    def _():
        wait_gather((i + 1) % GATHER_SLOTS)
        wait_gather((i + 2) % GATHER_SLOTS)


def _moe_down_kernel(be_ref, nu_ref, nxt_ref, par_ref, h_ref, wd_hbm, y_ref, wdb_ref, wdf_ref,
                     wsem_ref, *, layer):
    i = pl.program_id(0)
    _expert_weights(be_ref, nxt_ref, par_ref, layer,
                    [(wd_hbm, wdf_ref, wdb_ref, min(128, wdb_ref.shape[0]))], wsem_ref)

    @pl.when(i < nu_ref[0])
    def _():
        y_ref[...] = jnp.dot(h_ref[...], wdb_ref[...], preferred_element_type=F32)

    @pl.when(i >= nu_ref[0])
    def _():
        y_ref[...] = jnp.zeros_like(y_ref)


def _moe_slots_kernel(idx_ref, lt_ref, ut_ref, dest_ref, counts_ref, run_ref, start_ref, *, blk):
    p = pl.program_id(0)
    i = pl.program_id(1)
    tm, top_k = idx_ref.shape
    n_experts = run_ref.shape[-1]
    lanes = lax.broadcasted_iota(jnp.int32, (tm, n_experts), 1)
    idx = idx_ref[...]
    picks = [idx[:, k:k + 1] == lanes for k in range(top_k)]
    chosen = jnp.zeros((tm, n_experts), F32)
    for pk in picks:
        chosen = chosen + pk.astype(F32)
    tile_counts = jnp.sum(chosen, axis=0, keepdims=True)

    @pl.when(jnp.logical_and(p == 0, i == 0))
    def _():
        run_ref[...] = jnp.zeros_like(run_ref)

    @pl.when(p == 0)
    def _():
        run_ref[...] += tile_counts

    @pl.when(jnp.logical_and(p == 1, i == 0))
    def _():
        counts = run_ref[...].astype(jnp.int32)
        counts_ref[...] = counts
        padded = ((counts + (blk - 1)) // blk * blk).astype(F32)
        start_ref[...] = jnp.dot(jnp.broadcast_to(padded, (SUBLANES, n_experts)), ut_ref[...],
                                 preferred_element_type=F32, precision=lax.Precision.HIGHEST)[0:1]
        run_ref[...] = jnp.zeros_like(run_ref)

    @pl.when(p == 1)
    def _():
        earlier = jnp.dot(lt_ref[...], chosen.astype(BF16), preferred_element_type=F32)
        slot = earlier + run_ref[...] + start_ref[...]
        out = jnp.zeros((tm, n_experts), F32)
        for k, pk in enumerate(picks):
            mine = jnp.sum(jnp.where(pk, slot, 0.0), axis=1, keepdims=True)
            out = jnp.where(lanes == k, mine, out)
        dest_ref[...] = out.astype(jnp.int32)
        run_ref[...] += tile_counts


def _moe_slots(idx, n_experts, blk):
    t, top_k = idx.shape
    tm = _largest_tile(t, 256, SUBLANES)
    lt = jnp.asarray((np.arange(tm)[:, None] > np.arange(tm)[None, :]).astype(np.float32), BF16)
    ut = jnp.asarray((np.arange(n_experts)[:, None] < np.arange(n_experts)[None, :]).astype(np.float32))
    dest, counts = pl.pallas_call(
        functools.partial(_moe_slots_kernel, blk=blk),
        grid=(2, t // tm),
        in_specs=[pl.BlockSpec((tm, top_k), lambda p, i: (i, 0)),
                  pl.BlockSpec((tm, tm), lambda p, i: (0, 0)),
                  pl.BlockSpec((n_experts, n_experts), lambda p, i: (0, 0))],
        out_specs=[pl.BlockSpec((tm, n_experts), lambda p, i: (i * p, 0)),
                   pl.BlockSpec((1, n_experts), lambda p, i: (0, 0))],
        out_shape=[jax.ShapeDtypeStruct((t, n_experts), jnp.int32),
                   jax.ShapeDtypeStruct((1, n_experts), jnp.int32)],
        scratch_shapes=[pltpu.VMEM((1, n_experts), F32), pltpu.VMEM((1, n_experts), F32)],
        compiler_params=_params("arbitrary", "arbitrary"),
        name="moe_slots",
    )(idx.astype(jnp.int32), lt, ut)
    return dest[:, :top_k], counts[0]


def _routed_experts(x, idx, gate, layer, w_gate, w_up, w_down):
    t, d = x.shape
    n_experts, _, d_expert = w_gate.shape[1:]
    n = t * TOP_K
    blk = MOE_ROWS
    n_blocks = -(-n // blk) + n_experts
    dest, counts = _moe_slots(idx, n_experts, blk)
    ends = jnp.cumsum((counts + blk - 1) // blk * blk)
    tok = jnp.repeat(jnp.arange(t, dtype=jnp.int32), TOP_K)
    slot_tok = jnp.zeros((n_blocks * blk,), jnp.int32).at[dest.reshape(n)].set(tok)
    block_expert = jnp.minimum(jnp.searchsorted(ends, jnp.arange(n_blocks) * blk, side='right'),
                               n_experts - 1).astype(jnp.int32)
    n_used = (ends[-1] // blk).astype(jnp.int32).reshape(1)

    change = jnp.concatenate([jnp.ones((1,), bool), block_expert[1:] != block_expert[:-1]])
    par = ((jnp.cumsum(change) - 1) % 2).astype(jnp.int32)
    after = jnp.minimum(jnp.searchsorted(block_expert, block_expert, side='right'), n_blocks - 1)
    nxt = block_expert[after]
    hbm = pl.BlockSpec(memory_space=pl.ANY)

    hmid = pl.pallas_call(
        functools.partial(_moe_up_kernel, layer=layer),
        grid_spec=pltpu.PrefetchScalarGridSpec(
            num_scalar_prefetch=5,
            grid=(n_blocks,),
            in_specs=[hbm, hbm, hbm],
            out_specs=pl.BlockSpec((blk, d_expert), lambda i, *_: (i, 0)),
            scratch_shapes=[pltpu.VMEM((d, d_expert), BF16), pltpu.VMEM((d, d_expert), BF16),
                            pltpu.VMEM((2, d, d_expert), F32), pltpu.VMEM((2, d, d_expert), F32),
                            pltpu.VMEM((GATHER_SLOTS, blk, d), F32),
                            pltpu.SemaphoreType.DMA((GATHER_SLOTS,)),
                            pltpu.SemaphoreType.DMA((2, 2))]),
        out_shape=jax.ShapeDtypeStruct((n_blocks * blk, d_expert), BF16),
        compiler_params=_params("arbitrary"),
        name="moe_gate_up",
    )(block_expert, n_used, slot_tok, nxt, par, x, w_gate, w_up)

    y = pl.pallas_call(
        functools.partial(_moe_down_kernel, layer=layer),
        grid_spec=pltpu.PrefetchScalarGridSpec(
            num_scalar_prefetch=4,
            grid=(n_blocks,),
            in_specs=[pl.BlockSpec((blk, d_expert), lambda i, *_: (i, 0)), hbm],
            out_specs=pl.BlockSpec((blk, d), lambda i, *_: (i, 0)),
            scratch_shapes=[pltpu.VMEM((d_expert, d), BF16), pltpu.VMEM((2, d_expert, d), F32),
                            pltpu.SemaphoreType.DMA((1, 2))]),
        out_shape=jax.ShapeDtypeStruct((n_blocks * blk, d), F32),
        compiler_params=_params("arbitrary"),
        name="moe_down",
    )(block_expert, n_used, nxt, par, hmid, w_down)

    return jnp.sum(y[dest] * gate[:, :, None], axis=1)


def _shared_ln_kernel(xb_ref, h_ref, r_ref, sg_ref, su_ref, sd_ref, g_ref, b_ref, o_ref, ob_ref, *, alpha):
    x = xb_ref[...]
    mid = _silu(jnp.dot(x, sg_ref[...], preferred_element_type=F32)) * jnp.dot(
        x, su_ref[...], preferred_element_type=F32)
    shared = jnp.dot(mid.astype(BF16), sd_ref[...], preferred_element_type=F32)
    y = _layer_norm_rows(alpha * h_ref[...] + (r_ref[...] + shared), g_ref[...], b_ref[...])
    o_ref[...] = y
    ob_ref[...] = y.astype(BF16)


def _shared_ln(h, hb, routed, s_gate, s_up, s_down, g, b, alpha):
    t, d = h.shape
    tm = _largest_tile(t, 256, SUBLANES)
    row = pl.BlockSpec((tm, d), lambda i: (i, 0))
    vec = pl.BlockSpec((1, d), lambda i: (0, 0))

    def full(w):
        return pl.BlockSpec(w.shape, lambda i: (0, 0))

    return pl.pallas_call(
        functools.partial(_shared_ln_kernel, alpha=alpha),
        grid=(t // tm,),
        in_specs=[row, row, row, full(s_gate), full(s_up), full(s_down), vec, vec],
        out_specs=[row, row],
        out_shape=[jax.ShapeDtypeStruct((t, d), F32), jax.ShapeDtypeStruct((t, d), BF16)],
        compiler_params=_params("parallel"),
        name="shared_expert_layernorm",
    )(hb, h, routed, s_gate, s_up, s_down, g.reshape(1, d), b.reshape(1, d))


def kernel(x_prompt, x_sample, mem_prompt, cache_ssm_state, cache_conv, cache_sb_k, cache_sb_v, cache_mem_k, cache_mem_v, page_table, ssd_w_in, ssd_conv_w, ssd_conv_b, ssd_dt_bias, ssd_a_log, ssd_d, ssd_norm_g, ssd_w_out, sb_w_qkv, sb_w_o, sb_logit_bias, mem_w_q, mem_w_kv, mem_w_o, router_w, router_bias, moe_w_gate, moe_w_up, moe_w_down, shared_w_gate, shared_w_up, shared_w_down, ln_g, ln_b):
    depth = ln_g.shape[0]
    alpha = (2 * depth) ** 0.25
    b, s, d = x_prompt.shape
    db, dl, _ = x_sample.shape
    n_mem = mem_prompt.shape[1]
    mem_heads = cache_mem_k.shape[3]
    mem_hd = mem_heads * cache_mem_k.shape[4]
    _, _, ssd_heads, ssd_hd, ssd_state = cache_ssm_state.shape
    inner = ssd_heads * ssd_hd
    conv_dim = cache_conv.shape[-1]
    sb_heads, sb_hd = cache_sb_k.shape[3:]
    tp, ts = b * s, db * SAMPLE_ROWS

    hp = x_prompt
    hs = jnp.pad(x_sample, ((0, 0), (0, SAMPLE_ROWS - dl), (0, 0)))
    mem_b = mem_prompt.astype(BF16).reshape(b * n_mem, d)

    ssm_p, conv_p, k_p, v_p, mk_p, mv_p = [], [], [], [], [], []
    ssm_s, conv_s, k_s, v_s = [], [], [], []
    for l in range(depth):
        j = l // N_MIXERS
        if l % N_MIXERS == 0:
            w_in = ssd_w_in[j].astype(BF16)
            w = (w_in[:, :inner], w_in[:, inner:inner + conv_dim], w_in[:, inner + conv_dim:],
                 ssd_conv_w[j], ssd_conv_b[j], ssd_dt_bias[j], ssd_a_log[j], ssd_d[j], ssd_norm_g[j],
                 ssd_w_out[j].astype(BF16))
            mix_p, conv_new, ssm_new = _ssd_mixer(
                hp, s, jnp.zeros((b, cache_conv.shape[2], conv_dim), F32),
                jnp.zeros((b, ssd_heads, ssd_hd, ssd_state), F32), *w)
            conv_p.append(conv_new)
            ssm_p.append(ssm_new)
            mix_s, conv_new, ssm_new = _ssd_mixer(hs, dl, cache_conv[j], cache_ssm_state[j], *w)
            conv_s.append(conv_new)
            ssm_s.append(ssm_new)
        else:
            w_qkv = sb_w_qkv[j].astype(BF16)
            w_o = sb_w_o[j].astype(BF16)
            qkv = _matmul(hp.astype(BF16).reshape(tp, d), w_qkv).reshape(b, s, 3 * d)
            att = _sb_prompt(qkv, sb_logit_bias[j], sb_heads)
            mix_p = _matmul(att.reshape(tp, d), w_o).reshape(b, s, d)
            qkv5 = qkv.reshape(b, s, 3, sb_heads, sb_hd)
            k_p.append(qkv5[:, :, 1])
            v_p.append(qkv5[:, :, 2])

            qkv_s = _matmul(hs.astype(BF16).reshape(ts, d), w_qkv)
            qkv_s = qkv_s.reshape(db, SAMPLE_ROWS, 3, sb_heads, sb_hd)[:, :dl]
            att_s = _sb_sample(qkv_s[:, :, 0], qkv_s[:, :, 1], qkv_s[:, :, 2], cache_sb_k, cache_sb_v,
                               j, page_table, sb_logit_bias[j])
            att_s = jnp.pad(att_s, ((0, 0), (0, SAMPLE_ROWS - dl), (0, 0))).astype(BF16)
            mix_s = _matmul(att_s.reshape(ts, d), w_o).reshape(db, SAMPLE_ROWS, d)
            k_s.append(qkv_s[:, :, 1])
            v_s.append(qkv_s[:, :, 2])

        kv = _matmul(mem_b, mem_w_kv[l].astype(BF16)).reshape(b, n_mem, 2, mem_hd)
        mk, mv = kv[:, :, 0], kv[:, :, 1]
        mk_p.append(mk.reshape(b, n_mem, mem_heads, -1))
        mv_p.append(mv.reshape(b, n_mem, mem_heads, -1))

        hp1, hp1b = _res_ln(hp.reshape(tp, d), mix_p.reshape(tp, d), ln_g[l, 0], ln_b[l, 0], alpha)
        hs1, hs1b = _res_ln(hs.reshape(ts, d), mix_s.reshape(ts, d), ln_g[l, 0], ln_b[l, 0], alpha)
        w_q = mem_w_q[l].astype(BF16)
        w_mo = mem_w_o[l].astype(BF16)
        hp2, hp2b = _mem_block(hp1.reshape(b, s, d), hp1b.reshape(b, s, d), mk.astype(BF16),
                               mv.astype(BF16), w_q, w_mo, ln_g[l, 1], ln_b[l, 1], mem_heads, alpha)
        hs2, hs2b = _mem_block(hs1.reshape(db, SAMPLE_ROWS, d), hs1b.reshape(db, SAMPLE_ROWS, d),
                               cache_mem_k[l].reshape(db, n_mem, mem_hd).astype(BF16),
                               cache_mem_v[l].reshape(db, n_mem, mem_hd).astype(BF16),
                               w_q, w_mo, ln_g[l, 1], ln_b[l, 1], mem_heads, alpha)

        x2 = jnp.concatenate([hp2.reshape(tp, d), hs2.reshape(ts, d)], axis=0)
        x2b = jnp.concatenate([hp2b.reshape(tp, d), hs2b.reshape(ts, d)], axis=0)
        idx, gate = _route(x2, router_w[l], router_bias[l])
        routed = _routed_experts(x2, idx, gate, l, moe_w_gate, moe_w_up, moe_w_down)
        h3, _ = _shared_ln(x2, x2b, routed, shared_w_gate[l].astype(BF16), shared_w_up[l].astype(BF16),
                           shared_w_down[l].astype(BF16), ln_g[l, 2], ln_b[l, 2], alpha)
        hp = h3[:tp].reshape(b, s, d)
        hs = h3[tp:].reshape(db, SAMPLE_ROWS, d)

    return (hp, hs[:, :dl], jnp.stack(ssm_p), jnp.stack(conv_p), jnp.stack(k_p), jnp.stack(v_p),
            jnp.stack(mk_p), jnp.stack(mv_p), jnp.stack(ssm_s), jnp.stack(conv_s), jnp.stack(k_s),
            jnp.stack(v_s))
```
